```python
import jax, jax.numpy as jnp
from jax import lax
import numpy as np

D_MODEL = 4096
BATCH = 1
SEQ = 16384
DEPTH = 2

N_MIXERS = 2
N_META = 16
BLOCK = 128
FRONT_PAD = BLOCK - N_META
LEAD = BLOCK
EPS = 1e-6
NEG = -1e30

FOX_HEADS = 32
FOX_HEAD_DIM = 128
FOX_WIDTH = FOX_HEADS * FOX_HEAD_DIM
FOX_IN = 4 * FOX_WIDTH + FOX_HEADS

SSM_EXPAND = 2
SSM_INNER = SSM_EXPAND * D_MODEL
SSM_HEAD_DIM = 64
SSM_HEADS = SSM_INNER // SSM_HEAD_DIM
SSM_GROUPS = 8
SSM_HPG = SSM_HEADS // SSM_GROUPS
SSM_STATE = 128
CONV_WIDTH = 4
CONV_DIM = SSM_INNER + 2 * SSM_GROUPS * SSM_STATE
SSM_IN = SSM_INNER + CONV_DIM + SSM_HEADS

kernel_name = "fox_mamba2_interleaved_meta_trunk"


def rms_norm(x, g):
    xf = x.astype(jnp.float32)
    y = xf * lax.rsqrt(jnp.mean(xf * xf, axis=-1, keepdims=True) + EPS)
    return (y * g.astype(jnp.float32)).astype(x.dtype)


def fox_mixer(h, valid, w_in, b_f, q_g, k_g, w_out):
    bsz, plen, _ = h.shape
    proj = h @ w_in
    q, k, v, gate, f_logit = jnp.split(
        proj, [FOX_WIDTH, 2 * FOX_WIDTH, 3 * FOX_WIDTH, 4 * FOX_WIDTH], axis=-1)
    q = rms_norm(q.reshape(bsz, plen, FOX_HEADS, FOX_HEAD_DIM), q_g)
    k = rms_norm(k.reshape(bsz, plen, FOX_HEADS, FOX_HEAD_DIM), k_g)
    v = v.reshape(bsz, plen, FOX_HEADS, FOX_HEAD_DIM)
    log_f = jax.nn.log_sigmoid(f_logit.astype(jnp.float32) + b_f.astype(jnp.float32))
    log_f = jnp.where(valid[None, :, None], log_f, 0.0)
    cum = jnp.cumsum(log_f, axis=1).transpose(0, 2, 1)
    nblk = plen // BLOCK
    q_blocks = q.reshape(bsz, nblk, BLOCK, FOX_HEADS, FOX_HEAD_DIM).transpose(1, 0, 2, 3, 4)
    cum_blocks = cum.reshape(bsz, FOX_HEADS, nblk, BLOCK).transpose(2, 0, 1, 3)
    key_pos = jnp.arange(plen)
    scale = FOX_HEAD_DIM ** -0.5

    def one_block(args):
        qi, cq, bi = args
        s = jnp.einsum('bqhd,bkhd->bhqk', qi, k, preferred_element_type=jnp.float32) * scale
        s = s + cq[..., :, None] - cum[:, :, None, :]
        q_pos = bi * BLOCK + jnp.arange(BLOCK)
        mask = (key_pos[None, :] <= q_pos[:, None]) & valid[None, :]
        s = jnp.where(mask[None, None], s, NEG)
        p = jax.nn.softmax(s, axis=-1)
        return jnp.einsum('bhqk,bkhd->bqhd', p.astype(v.dtype), v)

    o = lax.map(one_block, (q_blocks, cum_blocks, jnp.arange(nblk)))
    o = o.transpose(1, 0, 2, 3, 4).reshape(bsz, plen, FOX_WIDTH)
    return (o * jax.nn.silu(gate)) @ w_out


def ssd_chunk_step(state, inp):
    xc, dac, bc, cc = inp
    cum = jnp.cumsum(dac, axis=1)
    seg = cum[:, :, None] - cum[:, None, :]
    causal = jnp.tril(jnp.ones((BLOCK, BLOCK), dtype=bool))[None, :, :, None, None]
    decay = jnp.exp(jnp.where(causal, seg, -jnp.inf))
    cb = jnp.einsum('btgn,bsgn->btsg', cc, bc, preferred_element_type=jnp.float32)
    y_diag = jnp.einsum('btsg,btsgr,bsgrp->btgrp', cb, decay, xc.astype(jnp.float32))
    y_off = jnp.einsum('btgn,bgrpn,btgr->btgrp', cc.astype(jnp.float32), state, jnp.exp(cum))
    last = cum[:, -1]
    w = jnp.exp(last[:, None] - cum)
    new_state = state * jnp.exp(last)[..., None, None] + jnp.einsum(
        'bsgn,bsgr,bsgrp->bgrpn', bc.astype(jnp.float32), w, xc.astype(jnp.float32))
    return new_state, y_diag + y_off


def ssd_mixer(h, valid, w_in, conv_w, conv_b, dt_bias, a_log, d_skip, norm_g, w_out):
    bsz, plen, _ = h.shape
    proj = h @ w_in
    z, xbc, dt = jnp.split(proj, [SSM_INNER, SSM_INNER + CONV_DIM], axis=-1)
    conv = lax.conv_general_dilated(
        xbc, conv_w.reshape(CONV_WIDTH, 1, CONV_DIM).astype(xbc.dtype),
        window_strides=(1,), padding=[(CONV_WIDTH - 1, 0)],
        dimension_numbers=('NWC', 'WIO', 'NWC'), feature_group_count=CONV_DIM)
    xbc = jax.nn.silu(conv + conv_b) * valid[None, :, None].astype(xbc.dtype)
    xs, bm, cm = jnp.split(xbc, [SSM_INNER, SSM_INNER + SSM_GROUPS * SSM_STATE], axis=-1)
    xs = xs.reshape(bsz, plen, SSM_GROUPS, SSM_HPG, SSM_HEAD_DIM)
    bm = bm.reshape(bsz, plen, SSM_GROUPS, SSM_STATE)
    cm = cm.reshape(bsz, plen, SSM_GROUPS, SSM_STATE)
    dt = jax.nn.softplus(dt.astype(jnp.float32) + dt_bias.astype(jnp.float32))
    dt = dt.reshape(bsz, plen, SSM_GROUPS, SSM_HPG)
    a = -jnp.exp(a_log.astype(jnp.float32)).reshape(SSM_GROUPS, SSM_HPG)
    da = dt * a
    xdt = xs.astype(jnp.float32) * dt[..., None]
    nch = plen // BLOCK

    def to_chunks(t):
        return jnp.moveaxis(t.reshape((bsz, nch, BLOCK) + t.shape[2:]), 1, 0)

    state0 = jnp.zeros((bsz, SSM_GROUPS, SSM_HPG, SSM_HEAD_DIM, SSM_STATE), jnp.float32)
    _, ys = lax.scan(ssd_chunk_step, state0,
                     (to_chunks(xdt), to_chunks(da), to_chunks(bm), to_chunks(cm)))
    y = jnp.moveaxis(ys, 0, 1).reshape(bsz, plen, SSM_GROUPS, SSM_HPG, SSM_HEAD_DIM)
    y = y + d_skip.astype(jnp.float32).reshape(SSM_GROUPS, SSM_HPG)[..., None] * xs.astype(jnp.float32)
    y = y.reshape(bsz, plen, SSM_INNER) * jax.nn.silu(z.astype(jnp.float32))
    yg = y.reshape(bsz, plen, SSM_GROUPS, SSM_INNER // SSM_GROUPS)
    yg = yg * lax.rsqrt(jnp.mean(yg * yg, axis=-1, keepdims=True) + EPS)
    y = yg.reshape(bsz, plen, SSM_INNER) * norm_g.astype(jnp.float32)
    return y.astype(h.dtype) @ w_out


def setup_inputs(seed: int = 0) -> dict:
    key = jax.random.key(seed)
    ks = jax.random.split(key, 16)
    f32 = jnp.float32
    n_fox = (DEPTH + 1) // 2
    n_ssm = DEPTH // 2
    x = jax.random.normal(ks[0], (BATCH, SEQ, D_MODEL), f32)
    meta = jax.random.normal(ks[1], (N_META, D_MODEL), f32)
    norm_g = 1.0 + 0.02 * jax.random.normal(ks[2], (DEPTH, D_MODEL), f32)
    fox_w_in = jax.random.normal(ks[3], (n_fox, D_MODEL, FOX_IN), f32) * D_MODEL ** -0.5
    fox_b_f = jax.random.uniform(ks[4], (n_fox, FOX_HEADS), f32, 1.0, 5.0)
    fox_q_g = 1.0 + 0.02 * jax.random.normal(ks[5], (n_fox, FOX_HEAD_DIM), f32)
    fox_k_g = 1.0 + 0.02 * jax.random.normal(ks[6], (n_fox, FOX_HEAD_DIM), f32)
    fox_w_out = jax.random.normal(ks[7], (n_fox, FOX_WIDTH, D_MODEL), f32) * FOX_WIDTH ** -0.5
    ssm_w_in = jax.random.normal(ks[8], (n_ssm, D_MODEL, SSM_IN), f32) * D_MODEL ** -0.5
    ssm_conv_w = jax.random.normal(ks[9], (n_ssm, CONV_WIDTH, CONV_DIM), f32) * CONV_WIDTH ** -0.5
    ssm_conv_b = 0.02 * jax.random.normal(ks[10], (n_ssm, CONV_DIM), f32)
    dt0 = jnp.exp(jax.random.uniform(ks[11], (n_ssm, SSM_HEADS), f32,
                                     float(np.log(1e-3)), float(np.log(1e-1))))
    ssm_dt_bias = dt0 + jnp.log(-jnp.expm1(-dt0))
    ssm_a_log = jnp.log(jax.random.uniform(ks[12], (n_ssm, SSM_HEADS), f32, 1.0, 16.0))
    ssm_d = 1.0 + 0.02 * jax.random.normal(ks[13], (n_ssm, SSM_HEADS), f32)
    ssm_norm_g = 1.0 + 0.02 * jax.random.normal(ks[14], (n_ssm, SSM_INNER), f32)
    ssm_w_out = jax.random.normal(ks[15], (n_ssm, SSM_INNER, D_MODEL), f32) * SSM_INNER ** -0.5
    return {'x': x, 'meta': meta, 'norm_g': norm_g,
            'fox_w_in': fox_w_in, 'fox_b_f': fox_b_f, 'fox_q_g': fox_q_g, 'fox_k_g': fox_k_g,
            'fox_w_out': fox_w_out,
            'ssm_w_in': ssm_w_in, 'ssm_conv_w': ssm_conv_w, 'ssm_conv_b': ssm_conv_b,
            'ssm_dt_bias': ssm_dt_bias, 'ssm_a_log': ssm_a_log, 'ssm_d': ssm_d,
            'ssm_norm_g': ssm_norm_g, 'ssm_w_out': ssm_w_out}


def reference(x, meta, norm_g, fox_w_in, fox_b_f, fox_q_g, fox_k_g, fox_w_out,
              ssm_w_in, ssm_conv_w, ssm_conv_b, ssm_dt_bias, ssm_a_log, ssm_d,
              ssm_norm_g, ssm_w_out):
    bsz = x.shape[0]
    h = jnp.concatenate([
        jnp.zeros((bsz, FRONT_PAD, D_MODEL), x.dtype),
        jnp.broadcast_to(meta.astype(x.dtype)[None], (bsz, N_META, D_MODEL)),
        x], axis=1)
    plen = h.shape[1]
    valid = jnp.arange(plen) >= FRONT_PAD
    vmask = valid[None, :, None].astype(h.dtype)
    for i in range(DEPTH):
        hn = rms_norm(h, norm_g[i])
        j = i // N_MIXERS
        if i % N_MIXERS == 0:
            out = fox_mixer(hn, valid, fox_w_in[j], fox_b_f[j], fox_q_g[j], fox_k_g[j], fox_w_out[j])
        else:
            out = ssd_mixer(hn, valid, ssm_w_in[j], ssm_conv_w[j], ssm_conv_b[j], ssm_dt_bias[j],
                            ssm_a_log[j], ssm_d[j], ssm_norm_g[j], ssm_w_out[j])
        h = h + out.astype(h.dtype) * vmask
    return h[:, LEAD:, :]
```

```python
import functools

import jax
import jax.numpy as jnp
from jax import lax
from jax.experimental import pallas as pl
from jax.experimental.pallas import tpu as pltpu

D_MODEL = 4096
SEQ = 16384
N_META = 16
BLOCK = 128
FRONT_PAD = BLOCK - N_META
EPS = 1e-6
NEG = -1e30

FOX_HEADS = 32
FOX_HEAD_DIM = 128
FOX_WIDTH = FOX_HEADS * FOX_HEAD_DIM

SSM_INNER = 8192
SSM_HEAD_DIM = 64
SSM_HEADS = SSM_INNER // SSM_HEAD_DIM
SSM_GROUPS = 8
SSM_HPG = SSM_HEADS // SSM_GROUPS
SSM_STATE = 128
CONV_WIDTH = 4
GROUP_WIDTH = SSM_HPG * SSM_HEAD_DIM

LANES = 128
VMEM_LIMIT = 56 * 1024 * 1024

F32 = jnp.float32
BF16 = jnp.bfloat16


def _params(sem, vmem=VMEM_LIMIT):
    return pltpu.CompilerParams(dimension_semantics=sem, vmem_limit_bytes=vmem)


def _split3(x):
    hi = x.astype(BF16)
    r1 = x - hi.astype(F32)
    mid = r1.astype(BF16)
    lo = (r1 - mid.astype(F32)).astype(BF16)
    return hi, mid, lo


def _norm_side_kernel(h_ref, g_ref, w_ref, hn_ref, side_ref):
    x = h_ref[...]
    ms = jnp.mean(x * x, axis=-1, keepdims=True)
    hn = (x * lax.rsqrt(ms + EPS) * g_ref[...]).astype(BF16)
    hn_ref[...] = hn
    side_ref[...] = jnp.dot(hn, w_ref[...], preferred_element_type=F32)


def norm_side(h, g, w_side, tr=512):
    p, d = h.shape
    return pl.pallas_call(
        _norm_side_kernel,
        grid=(pl.cdiv(p, tr),),
        in_specs=[pl.BlockSpec((tr, d), lambda i: (i, 0)),
                  pl.BlockSpec((1, d), lambda i: (0, 0)),
                  pl.BlockSpec((d, LANES), lambda i: (0, 0))],
        out_specs=[pl.BlockSpec((tr, d), lambda i: (i, 0)),
                   pl.BlockSpec((tr, LANES), lambda i: (i, 0))],
        out_shape=[jax.ShapeDtypeStruct((p, d), BF16),
                   jax.ShapeDtypeStruct((p, LANES), F32)],
        compiler_params=_params(("parallel",)),
        name="norm_side",
    )(h, g, w_side)


def _fcum_kernel(side_ref, b_ref, tri_ref, cq_ref, ck_ref, carry_ref, *, nblk):
    i = pl.program_id(0)

    @pl.when(i == 0)
    def _():
        carry_ref[...] = jnp.zeros_like(carry_ref)

    row = lax.broadcasted_iota(jnp.int32, (BLOCK, LANES), 0)
    lane = lax.broadcasted_iota(jnp.int32, (BLOCK, LANES), 1)
    valid = jnp.logical_or(i > 0, row >= FRONT_PAD)
    lf = jax.nn.log_sigmoid(side_ref[...] + b_ref[...])
    lf = jnp.where(valid, lf, 0.0)
    hi, mid, lo = _split3(lf)
    tri = tri_ref[...]
    cum = (jnp.dot(tri, hi, preferred_element_type=F32)
           + jnp.dot(tri, mid, preferred_element_type=F32)
           + jnp.dot(tri, lo, preferred_element_type=F32)) + carry_ref[0:1, :]
    carry_ref[...] = jnp.broadcast_to(cum[BLOCK - 1:BLOCK, :], carry_ref.shape)

    def triple(v):
        vh, vm, vl = _split3(v)
        one = jnp.ones_like(vh)
        return jnp.where(lane < 32, vh, jnp.where(lane < 64, vm, jnp.where(lane < 96, vl, one)))

    cq_ref[...] = triple(cum)
    ck_ref[...] = triple(jnp.where(valid, -cum, NEG))


def fcum(side, b_rep):
    p = side.shape[0]
    nblk = p // BLOCK
    tri = jnp.tril(jnp.ones((BLOCK, BLOCK), BF16))
    blk = lambda i: ((i + nblk - 1) % nblk, 0)
    return pl.pallas_call(
        functools.partial(_fcum_kernel, nblk=nblk),
        grid=(nblk,),
        in_specs=[pl.BlockSpec((BLOCK, LANES), blk),
                  pl.BlockSpec((1, LANES), lambda i: (0, 0)),
                  pl.BlockSpec((BLOCK, BLOCK), lambda i: (0, 0))],
        out_specs=[pl.BlockSpec((BLOCK, LANES), blk),
                   pl.BlockSpec((BLOCK, LANES), blk)],
        out_shape=[jax.ShapeDtypeStruct((p, LANES), BF16),
                   jax.ShapeDtypeStruct((p, LANES), BF16)],
        scratch_shapes=[pltpu.VMEM((8, LANES), F32)],
        compiler_params=_params(("arbitrary",)),
        name="fcum",
    )(side, b_rep, tri)


def _mm_heads_kernel(x_ref, w_ref, o_ref):
    acc = jnp.dot(x_ref[...], w_ref[...], preferred_element_type=F32)
    for j in range(o_ref.shape[0]):
        o_ref[j] = acc[:, j * LANES:(j + 1) * LANES].astype(o_ref.dtype)


def mm_heads(x, w, col0, n, tm, tn):
    m, k = x.shape
    c0 = col0 // tn
    hb = tn // LANES
    return pl.pallas_call(
        _mm_heads_kernel,
        grid=(pl.cdiv(m, tm), n // tn),
        in_specs=[pl.BlockSpec((tm, k), lambda i, j: (i, 0)),
                  pl.BlockSpec((k, tn), lambda i, j: (0, c0 + j))],
        out_specs=pl.BlockSpec((hb, tm, LANES), lambda i, j: (j, i, 0)),
        out_shape=jax.ShapeDtypeStruct((n // LANES, m, LANES), BF16),
        compiler_params=_params(("parallel", "parallel")),
        name="mm_heads",
    )(x, w)


def _mm_qk_kernel(x_ref, w_ref, g_ref, c3_ref, sel_ref, o_ref, *, scale):
    acc = jnp.dot(x_ref[...], w_ref[...], preferred_element_type=F32)
    c3 = c3_ref[...]
    g = g_ref[...] * scale
    for j in range(o_ref.shape[0]):
        a = acc[:, j * LANES:(j + 1) * LANES]
        ms = jnp.mean(a * a, axis=-1, keepdims=True)
        o_ref[j, :, 0:LANES] = (a * lax.rsqrt(ms + EPS) * g).astype(BF16)
        o_ref[j, :, LANES:2 * LANES] = jnp.dot(
            c3, sel_ref[j], preferred_element_type=F32).astype(BF16)


def mm_qk(x, w, col0, g, c3, sel, scale, tm, tn):
    m, k = x.shape
    c0 = col0 // tn
    hb = tn // LANES
    nh = sel.shape[0]
    return pl.pallas_call(
        functools.partial(_mm_qk_kernel, scale=scale),
        grid=(pl.cdiv(m, tm), nh // hb),
        in_specs=[pl.BlockSpec((tm, k), lambda i, j: (i, 0)),
                  pl.BlockSpec((k, tn), lambda i, j: (0, c0 + j)),
                  pl.BlockSpec((1, LANES), lambda i, j: (0, 0)),
                  pl.BlockSpec((tm, LANES), lambda i, j: (i, 0)),
                  pl.BlockSpec((hb, LANES, LANES), lambda i, j: (j, 0, 0))],
        out_specs=pl.BlockSpec((hb, tm, 2 * LANES), lambda i, j: (j, i, 0)),
        out_shape=jax.ShapeDtypeStruct((nh, m, 2 * LANES), BF16),
        compiler_params=_params(("parallel", "parallel")),
        name="mm_qk",
    )(x, w, g, c3, sel)


def _mm_plain_kernel(x_ref, w_ref, o_ref):
    o_ref[...] = jnp.dot(x_ref[...], w_ref[...], preferred_element_type=F32).astype(o_ref.dtype)


def mm_plain(x, w, n, tm, tn, out_dtype=BF16):
    m, k = x.shape
    return pl.pallas_call(
        _mm_plain_kernel,
        grid=(pl.cdiv(m, tm), n // tn),
        in_specs=[pl.BlockSpec((tm, k), lambda i, j: (i, 0)),
                  pl.BlockSpec((k, tn), lambda i, j: (0, j))],
        out_specs=pl.BlockSpec((tm, tn), lambda i, j: (i, j)),
        out_shape=jax.ShapeDtypeStruct((m, n), out_dtype),
        compiler_params=_params(("parallel", "parallel")),
        name="mm_plain",
    )(x, w)


def _mm_res_kernel(x_ref, w_ref, r_ref, o_ref, acc_ref, *, tm, nk, first_valid, mask_rows):
    kk = pl.program_id(2)
    part = jnp.dot(x_ref[...], w_ref[...], preferred_element_type=F32)

    def finish(total):
        if mask_rows:
            row = pl.program_id(0) * tm + lax.broadcasted_iota(jnp.int32, total.shape, 0)
            keep = jnp.logical_or(row < SEQ, row >= first_valid)
            total = jnp.where(keep, total, 0.0)
        o_ref[...] = r_ref[...] + total

    if nk == 1:
        finish(part)
    else:
        @pl.when(kk == 0)
        def _():
            acc_ref[...] = part

        @pl.when(jnp.logical_and(kk > 0, kk < nk - 1))
        def _():
            acc_ref[...] += part

        @pl.when(kk == nk - 1)
        def _():
            finish(acc_ref[...] + part)


def mm_residual(x, w, res, m, tm, tn, tk, mask_rows):
    k = x.shape[1]
    n = w.shape[1]
    nk = k // tk
    return pl.pallas_call(
        functools.partial(_mm_res_kernel, tm=tm, nk=nk, first_valid=SEQ + FRONT_PAD,
                          mask_rows=mask_rows),
        grid=(pl.cdiv(m, tm), n // tn, nk),
        in_specs=[pl.BlockSpec((tm, tk), lambda i, j, kk: (i, kk)),
                  pl.BlockSpec((tk, tn), lambda i, j, kk: (kk, j)),
                  pl.BlockSpec((tm, tn), lambda i, j, kk: (i, j))],
        out_specs=pl.BlockSpec((tm, tn), lambda i, j, kk: (i, j)),
        out_shape=jax.ShapeDtypeStruct((m, n), F32),
        scratch_shapes=[pltpu.VMEM((tm, tn), F32)],
        compiler_params=_params(("parallel", "parallel", "arbitrary")),
        name="mm_residual",
    )(x, w, res)


def _attn_kernel(q_ref, k_ref, v_ref, gate_ref, o_ref, *, bq, nq):
    i = pl.program_id(1)
    q = q_ref[0]
    dn = (((1,), (1,)), ((), ()))

    def update(carry, kc, vc, mask):
        m, l, acc = carry
        s = lax.dot_general(q, kc, dn, preferred_element_type=F32)
        if mask is not None:
            s = jnp.where(mask, s, NEG)
        m_new = jnp.maximum(m, jnp.max(s, axis=-1, keepdims=True))
        alpha = jnp.exp(m - m_new)
        p = jnp.exp(s - m_new)
        l = alpha * l + jnp.sum(p, axis=-1, keepdims=True)
        acc = alpha * acc + jnp.dot(p.astype(BF16), vc, preferred_element_type=F32)
        return m_new, l, acc

    def finish(carry):
        _, l, acc = carry
        gate = gate_ref[0].astype(F32)
        o_ref[...] = (acc / l * (gate * jax.nn.sigmoid(gate))).astype(o_ref.dtype)

    init = (jnp.full((bq, 1), NEG, F32), jnp.zeros((bq, 1), F32),
            jnp.zeros((bq, FOX_HEAD_DIM), F32))
    kp = k_ref[0, SEQ:SEQ + BLOCK, :]
    vp = v_ref[0, SEQ:SEQ + BLOCK, :]

    @pl.when(i < nq)
    def _():
        carry = update(init, kp, vp, None)

        def body(j, c):
            off = pl.multiple_of(j * bq, bq)
            return update(c, k_ref[0, pl.ds(off, bq), :], v_ref[0, pl.ds(off, bq), :], None)

        carry = lax.fori_loop(0, i, body, carry)
        off = pl.multiple_of(i * bq, bq)
        tri = (lax.broadcasted_iota(jnp.int32, (bq, bq), 1)
               <= lax.broadcasted_iota(jnp.int32, (bq, bq), 0))
        carry = update(carry, k_ref[0, pl.ds(off, bq), :], v_ref[0, pl.ds(off, bq), :], tri)
        finish(carry)

    @pl.when(i == nq)
    def _():
        tri = (lax.broadcasted_iota(jnp.int32, (bq, BLOCK), 1)
               <= lax.broadcasted_iota(jnp.int32, (bq, BLOCK), 0))
        finish(update(init, kp, vp, tri))


def fox_attention(q, k, v, gate, bq=512):
    nh, p, _ = q.shape
    nq = SEQ // bq
    return pl.pallas_call(
        functools.partial(_attn_kernel, bq=bq, nq=nq),
        grid=(nh, nq + 1),
        in_specs=[pl.BlockSpec((1, bq, 2 * LANES), lambda h, i: (h, i, 0)),
                  pl.BlockSpec((1, p, 2 * LANES), lambda h, i: (h, 0, 0)),
                  pl.BlockSpec((1, p, LANES), lambda h, i: (h, 0, 0)),
                  pl.BlockSpec((1, bq, LANES), lambda h, i: (h, i, 0))],
        out_specs=pl.BlockSpec((bq, LANES), lambda h, i: (i, h)),
        out_shape=jax.ShapeDtypeStruct((p, nh * LANES), BF16),
        compiler_params=_params(("parallel", "arbitrary")),
        name="fox_attention",
    )(q, k, v, gate)


def _conv_silu(raw_ref, tail_ref, w_ref, b_ref, valid):
    raw = raw_ref[...].astype(F32)
    u = jnp.concatenate([tail_ref[...], raw], axis=0)
    acc = b_ref[...] + w_ref[CONV_WIDTH - 1:CONV_WIDTH, :] * raw
    for s in range(1, CONV_WIDTH):
        acc = acc + w_ref[CONV_WIDTH - 1 - s:CONV_WIDTH - s, :] * u[8 - s:8 - s + BLOCK, :]
    tail_ref[...] = raw[BLOCK - 8:BLOCK, :]
    return jnp.where(valid, acc * jax.nn.sigmoid(acc), 0.0)


def _ssd_kernel(z_ref, x_ref, b_ref, c_ref, dtt_ref, alog_ref, wx_ref, wb_ref, wc_ref,
                bx_ref, bb_ref, bc_ref, dsk_ref, ng_ref, y_ref,
                st_ref, y_scr, tx_ref, tb_ref, tc_ref):
    c = pl.program_id(1)

    @pl.when(c == 0)
    def _():
        st_ref[...] = jnp.zeros_like(st_ref)
        tx_ref[...] = jnp.zeros_like(tx_ref)
        tb_ref[...] = jnp.zeros_like(tb_ref)
        tc_ref[...] = jnp.zeros_like(tc_ref)

    row1 = lax.broadcasted_iota(jnp.int32, (BLOCK, 1), 0)
    valid = jnp.logical_or(c > 0, row1 >= FRONT_PAD)
    xs = _conv_silu(x_ref, tx_ref, wx_ref, bx_ref, valid)
    bm = _conv_silu(b_ref, tb_ref, wb_ref, bb_ref, valid)
    cm = _conv_silu(c_ref, tc_ref, wc_ref, bc_ref, valid)
    xs_b = xs.astype(BF16)
    bm_b = bm.astype(BF16)
    cm_b = cm.astype(BF16)

    dtt = dtt_ref[...]
    dat = dtt * (-jnp.exp(alog_ref[...]))
    upper = (lax.broadcasted_iota(jnp.int32, (BLOCK, BLOCK), 0)
             <= lax.broadcasted_iota(jnp.int32, (BLOCK, BLOCK), 1)).astype(BF16)
    hi, mid, lo = _split3(dat)
    cumt = (jnp.dot(hi, upper, preferred_element_type=F32)
            + jnp.dot(mid, upper, preferred_element_type=F32)
            + jnp.dot(lo, upper, preferred_element_type=F32))
    last = cumt[:, BLOCK - 1:BLOCK]
    wdt = jnp.exp(last - cumt) * dtt

    cb = lax.dot_general(cm_b, bm_b, (((1,), (1,)), ((), ())), preferred_element_type=F32)
    bmt_f = bm.T
    causal = (lax.broadcasted_iota(jnp.int32, (BLOCK, BLOCK), 1)
              <= lax.broadcasted_iota(jnp.int32, (BLOCK, BLOCK), 0))
    lane = lax.broadcasted_iota(jnp.int32, (BLOCK, LANES), 1)
    left = lane < SSM_HEAD_DIM

    for pr in range(SSM_HPG // 2):
        lhs = []
        cs = []
        elast = []
        for hh in (2 * pr, 2 * pr + 1):
            rh = jnp.broadcast_to(cumt[hh:hh + 1, :], (BLOCK, BLOCK))
            ch = rh.T
            decay = jnp.exp(jnp.where(causal, ch - rh, -jnp.inf))
            lhs.append((cb * decay * dtt[hh:hh + 1, :]).astype(BF16))
            lhs.append((bmt_f * wdt[hh:hh + 1, :]).astype(BF16))
            ech = jnp.exp(ch)
            cs.append((cm * ech).astype(BF16))
            elast.append(ech[BLOCK - 1:BLOCK, :])
        cols = slice(pr * LANES, (pr + 1) * LANES)
        st_old = st_ref[:, cols]
        r1 = jnp.dot(jnp.concatenate(lhs, axis=0), xs_b[:, cols], preferred_element_type=F32)
        r2 = jnp.dot(jnp.concatenate(cs, axis=0), st_old.astype(BF16),
                     preferred_element_type=F32)
        y_scr[:, cols] = jnp.where(left, r1[0:BLOCK] + r2[0:BLOCK],
                                   r1[2 * BLOCK:3 * BLOCK] + r2[BLOCK:2 * BLOCK])
        ds = jnp.where(left, r1[BLOCK:2 * BLOCK], r1[3 * BLOCK:4 * BLOCK])
        el = jnp.where(left[0:1, :], elast[0], elast[1])
        st_ref[:, cols] = st_old * el + ds

    z = z_ref[...].astype(F32)
    y = y_scr[...] + dsk_ref[...] * xs
    y = y * (z * jax.nn.sigmoid(z))
    y = y * lax.rsqrt(jnp.mean(y * y, axis=-1, keepdims=True) + EPS)
    y_ref[...] = (y * ng_ref[...]).astype(y_ref.dtype)


def ssd_scan(zx, dtt, alog_col, conv_w, conv_b, dsk_rep, norm_g):
    p = zx.shape[0]
    nch = p // BLOCK
    gw = GROUP_WIDTH
    zb = SSM_INNER // gw
    bb = 2 * SSM_INNER // SSM_STATE
    cb = bb + SSM_GROUPS
    cwx = 0
    cwb = SSM_INNER // SSM_STATE
    cwc = cwb + SSM_GROUPS
    rows = lambda c: (c + nch - 1) % nch
    return pl.pallas_call(
        _ssd_kernel,
        grid=(SSM_GROUPS, nch),
        in_specs=[
            pl.BlockSpec((BLOCK, gw), lambda g, c: (rows(c), g)),
            pl.BlockSpec((BLOCK, gw), lambda g, c: (rows(c), zb + g)),
            pl.BlockSpec((BLOCK, SSM_STATE), lambda g, c: (rows(c), bb + g)),
            pl.BlockSpec((BLOCK, SSM_STATE), lambda g, c: (rows(c), cb + g)),
            pl.BlockSpec((SSM_HPG, BLOCK), lambda g, c: (g, rows(c))),
            pl.BlockSpec((SSM_HPG, 1), lambda g, c: (g, 0)),
            pl.BlockSpec((CONV_WIDTH, gw), lambda g, c: (0, cwx + g)),
            pl.BlockSpec((CONV_WIDTH, SSM_STATE), lambda g, c: (0, cwb + g)),
            pl.BlockSpec((CONV_WIDTH, SSM_STATE), lambda g, c: (0, cwc + g)),
            pl.BlockSpec((1, gw), lambda g, c: (0, cwx + g)),
            pl.BlockSpec((1, SSM_STATE), lambda g, c: (0, cwb + g)),
            pl.BlockSpec((1, SSM_STATE), lambda g, c: (0, cwc + g)),
            pl.BlockSpec((1, gw), lambda g, c: (0, g)),
            pl.BlockSpec((1, gw), lambda g, c: (0, g)),
        ],
        out_specs=pl.BlockSpec((BLOCK, gw), lambda g, c: (rows(c), g)),
        out_shape=jax.ShapeDtypeStruct((p, SSM_INNER), BF16),
        scratch_shapes=[pltpu.VMEM((SSM_STATE, gw), F32),
                        pltpu.VMEM((BLOCK, gw), F32),
                        pltpu.VMEM((8, gw), F32),
                        pltpu.VMEM((8, SSM_STATE), F32),
                        pltpu.VMEM((8, SSM_STATE), F32)],
        compiler_params=_params(("parallel", "arbitrary")),
        name="ssd_scan",
    )(zx, zx, zx, zx, dtt, alog_col, conv_w, conv_w, conv_w, conv_b, conv_b, conv_b,
      dsk_rep, norm_g)


def _dt_kernel(raw_ref, b_ref, o_ref):
    o_ref[...] = jax.nn.softplus(raw_ref[...] + b_ref[...]).T


def dt_transpose(raw, bias, tr=512):
    p = raw.shape[0]
    return pl.pallas_call(
        _dt_kernel,
        grid=(pl.cdiv(p, tr),),
        in_specs=[pl.BlockSpec((tr, LANES), lambda i: (i, 0)),
                  pl.BlockSpec((1, LANES), lambda i: (0, 0))],
        out_specs=pl.BlockSpec((LANES, tr), lambda i: (0, i)),
        out_shape=jax.ShapeDtypeStruct((LANES, p), F32),
        compiler_params=_params(("parallel",)),
        name="dt_transpose",
    )(raw, bias)


def _bias_selectors():
    h = jnp.arange(FOX_HEADS)
    r = jnp.arange(LANES)[None, :, None]
    c = jnp.arange(LANES)[None, None, :]
    hh = h[:, None, None]
    own = (r == hh + 32 * c) & (c < 3)
    ones_q = (r == 96) & (c >= 3) & (c < 6)
    sel_q = (own | ones_q).astype(BF16)
    own_k = (r == hh + 32 * (c - 3)) & (c >= 3) & (c < 6)
    ones_k = (r == 96) & (c < 3)
    sel_k = (own_k | ones_k).astype(BF16)
    return sel_q, sel_k


def kernel(x, meta, norm_g, fox_w_in, fox_b_f, fox_q_g, fox_k_g, fox_w_out, ssm_w_in, ssm_conv_w,
           ssm_conv_b, ssm_dt_bias, ssm_a_log, ssm_d, ssm_norm_g, ssm_w_out):
    p = SEQ + BLOCK
    h0 = jnp.concatenate([x[0], jnp.zeros((FRONT_PAD, D_MODEL), x.dtype),
                          meta.astype(x.dtype)], axis=0)

    w_in = fox_w_in[0].astype(BF16)
    w_f = fox_w_in[0][:, 4 * FOX_WIDTH:]
    w_f = jnp.concatenate([w_f, w_f, w_f, jnp.zeros_like(w_f)], axis=1).astype(BF16)
    b_f = fox_b_f[0]
    b_f = jnp.concatenate([b_f, b_f, b_f, jnp.zeros_like(b_f)])[None, :]
    hn, f_side = norm_side(h0, norm_g[0][None, :], w_f)
    cq, ck = fcum(f_side, b_f)
    sel_q, sel_k = _bias_selectors()
    tm, tn = 1376, 512
    q = mm_qk(hn, w_in, 0, fox_q_g[0][None, :], cq, sel_q, FOX_HEAD_DIM ** -0.5, tm, tn)
    k = mm_qk(hn, w_in, FOX_WIDTH, fox_k_g[0][None, :], ck, sel_k, 1.0, tm, tn)
    v = mm_heads(hn, w_in, 2 * FOX_WIDTH, FOX_WIDTH, tm, tn)
    gate = mm_heads(hn, w_in, 3 * FOX_WIDTH, FOX_WIDTH, tm, tn)
    og = fox_attention(q, k, v, gate)
    h1 = mm_residual(og, fox_w_out[0].astype(BF16), h0, p, tm, tn, FOX_WIDTH, True)

    w_in = ssm_w_in[0]
    n_zx = 2 * SSM_INNER + 2 * SSM_GROUPS * SSM_STATE
    hn, dt_raw = norm_side(h1, norm_g[1][None, :], w_in[:, n_zx:].astype(BF16))
    dtt = dt_transpose(dt_raw, ssm_dt_bias[0][None, :])
    zx = mm_plain(hn, w_in.astype(BF16), n_zx, tm, tn)
    dsk_rep = jnp.repeat(ssm_d[0], SSM_HEAD_DIM)[None, :]
    y = ssd_scan(zx, dtt, ssm_a_log[0][:, None], ssm_conv_w[0], ssm_conv_b[0][None, :],
                 dsk_rep, ssm_norm_g[0][None, :])
    out = mm_residual(y, ssm_w_out[0].astype(BF16), h1, SEQ, 1024, 512, 4096, False)
    return out[None]
```

```python
import functools

import jax
import jax.numpy as jnp
from jax import lax
from jax.experimental import pallas as pl
from jax.experimental.pallas import tpu as pltpu

D_MODEL = 4096
SEQ = 16384
N_META = 16
BLOCK = 128
FRONT_PAD = BLOCK - N_META
EPS = 1e-6
NEG = -1e30

FOX_HEADS = 32
FOX_HEAD_DIM = 128
FOX_WIDTH = FOX_HEADS * FOX_HEAD_DIM

SSM_INNER = 8192
SSM_HEAD_DIM = 64
SSM_HEADS = SSM_INNER // SSM_HEAD_DIM
SSM_GROUPS = 8
SSM_HPG = SSM_HEADS // SSM_GROUPS
SSM_STATE = 128
CONV_WIDTH = 4
GROUP_WIDTH = SSM_HPG * SSM_HEAD_DIM

LOG2E = 1.4426950408889634
SAFE_LOG2_RANGE = 96.0

LANES = 128
VMEM_LIMIT = 56 * 1024 * 1024

F32 = jnp.float32
BF16 = jnp.bfloat16


def _params(sem, vmem=VMEM_LIMIT):
    return pltpu.CompilerParams(dimension_semantics=sem, vmem_limit_bytes=vmem)


def _silu(x):
    h = 0.5 * x
    return h * (jnp.tanh(h) + 1.0)


def _split3(x):
    hi = x.astype(BF16)
    r1 = x - hi.astype(F32)
    mid = r1.astype(BF16)
    lo = (r1 - mid.astype(F32)).astype(BF16)
    return hi, mid, lo


def _norm_side_kernel(h_ref, g_ref, w_ref, hn_ref, side_ref):
    x = h_ref[...]
    ms = jnp.mean(x * x, axis=-1, keepdims=True)
    hn = (x * lax.rsqrt(ms + EPS) * g_ref[...]).astype(BF16)
    hn_ref[...] = hn
    side_ref[...] = jnp.dot(hn, w_ref[...], preferred_element_type=F32)


def norm_side(h, g, w_side, tr=512):
    p, d = h.shape
    return pl.pallas_call(
        _norm_side_kernel,
        grid=(pl.cdiv(p, tr),),
        in_specs=[pl.BlockSpec((tr, d), lambda i: (i, 0)),
                  pl.BlockSpec((1, d), lambda i: (0, 0)),
                  pl.BlockSpec((d, LANES), lambda i: (0, 0))],
        out_specs=[pl.BlockSpec((tr, d), lambda i: (i, 0)),
                   pl.BlockSpec((tr, LANES), lambda i: (i, 0))],
        out_shape=[jax.ShapeDtypeStruct((p, d), BF16),
                   jax.ShapeDtypeStruct((p, LANES), F32)],
        compiler_params=_params(("parallel",)),
        name="norm_side",
    )(h, g, w_side)


def _fcum_kernel(side_ref, b_ref, shift_ref, tri_ref, cq_ref, ck_ref, carry_ref):
    i = pl.program_id(0)

    @pl.when(i == 0)
    def _():
        carry_ref[...] = jnp.zeros_like(carry_ref)

    row = lax.broadcasted_iota(jnp.int32, (BLOCK, LANES), 0)
    lane = lax.broadcasted_iota(jnp.int32, (BLOCK, LANES), 1)
    valid = jnp.logical_or(i > 0, row >= FRONT_PAD)
    lf = jax.nn.log_sigmoid(side_ref[...] + b_ref[...])
    lf = jnp.where(valid, lf, 0.0)
    hi, mid, lo = _split3(lf)
    tri = tri_ref[...]
    cum = (jnp.dot(tri, hi, preferred_element_type=F32)
           + jnp.dot(tri, mid, preferred_element_type=F32)
           + jnp.dot(tri, lo, preferred_element_type=F32)) + carry_ref[0:1, :]
    carry_ref[...] = jnp.broadcast_to(cum[BLOCK - 1:BLOCK, :], carry_ref.shape)

    def triple(v):
        vh, vm, vl = _split3(v)
        one = jnp.ones_like(vh)
        return jnp.where(lane < 32, vh, jnp.where(lane < 64, vm, jnp.where(lane < 96, vl, one)))

    cum2 = cum * LOG2E
    cq_ref[...] = triple(cum2 - shift_ref[...])
    ck_ref[...] = triple(jnp.where(valid, -cum2, NEG))


def fcum(side, b_rep, shift):
    p = side.shape[0]
    nblk = p // BLOCK
    tri = jnp.tril(jnp.ones((BLOCK, BLOCK), BF16))
    blk = lambda i: ((i + nblk - 1) % nblk, 0)
    return pl.pallas_call(
        _fcum_kernel,
        grid=(nblk,),
        in_specs=[pl.BlockSpec((BLOCK, LANES), blk),
                  pl.BlockSpec((1, LANES), lambda i: (0, 0)),
                  pl.BlockSpec((1, LANES), lambda i: (0, 0)),
                  pl.BlockSpec((BLOCK, BLOCK), lambda i: (0, 0))],
        out_specs=[pl.BlockSpec((BLOCK, LANES), blk),
                   pl.BlockSpec((BLOCK, LANES), blk)],
        out_shape=[jax.ShapeDtypeStruct((p, LANES), BF16),
                   jax.ShapeDtypeStruct((p, LANES), BF16)],
        scratch_shapes=[pltpu.VMEM((8, LANES), F32)],
        compiler_params=_params(("arbitrary",)),
        name="fcum",
    )(side, b_rep, shift, tri)


def _mm_heads_kernel(x_ref, w_ref, o_ref, *, ones_pad):
    acc = jnp.dot(x_ref[...], w_ref[...], preferred_element_type=F32)
    for j in range(o_ref.shape[0]):
        o_ref[j, :, 0:LANES] = acc[:, j * LANES:(j + 1) * LANES].astype(o_ref.dtype)
        if ones_pad:
            o_ref[j, :, LANES:2 * LANES] = jnp.ones((o_ref.shape[1], LANES), o_ref.dtype)


def mm_heads(x, w, col0, n, tm, tn, ones_pad=False):
    m, k = x.shape
    c0 = col0 // tn
    hb = tn // LANES
    wo = 2 * LANES if ones_pad else LANES
    return pl.pallas_call(
        functools.partial(_mm_heads_kernel, ones_pad=ones_pad),
        grid=(pl.cdiv(m, tm), n // tn),
        in_specs=[pl.BlockSpec((tm, k), lambda i, j: (i, 0)),
                  pl.BlockSpec((k, tn), lambda i, j: (0, c0 + j))],
        out_specs=pl.BlockSpec((hb, tm, wo), lambda i, j: (j, i, 0)),
        out_shape=jax.ShapeDtypeStruct((n // LANES, m, wo), BF16),
        compiler_params=_params(("parallel", "parallel")),
        name="mm_heads",
    )(x, w)


def _mm_qk_kernel(x_ref, w_ref, g_ref, c3_ref, sel_ref, o_ref, *, scale):
    acc = jnp.dot(x_ref[...], w_ref[...], preferred_element_type=F32)
    c3 = c3_ref[...]
    g = g_ref[...] * scale
    for j in range(o_ref.shape[0]):
        a = acc[:, j * LANES:(j + 1) * LANES]
        ms = jnp.mean(a * a, axis=-1, keepdims=True)
        o_ref[j, :, 0:LANES] = (a * lax.rsqrt(ms + EPS) * g).astype(BF16)
        o_ref[j, :, LANES:2 * LANES] = jnp.dot(
            c3, sel_ref[j], preferred_element_type=F32).astype(BF16)


def mm_qk(x, w, col0, g, c3, sel, scale, tm, tn):
    m, k = x.shape
    c0 = col0 // tn
    hb = tn // LANES
    nh = sel.shape[0]
    return pl.pallas_call(
        functools.partial(_mm_qk_kernel, scale=scale),
        grid=(pl.cdiv(m, tm), nh // hb),
        in_specs=[pl.BlockSpec((tm, k), lambda i, j: (i, 0)),
                  pl.BlockSpec((k, tn), lambda i, j: (0, c0 + j)),
                  pl.BlockSpec((1, LANES), lambda i, j: (0, 0)),
                  pl.BlockSpec((tm, LANES), lambda i, j: (i, 0)),
                  pl.BlockSpec((hb, LANES, LANES), lambda i, j: (j, 0, 0))],
        out_specs=pl.BlockSpec((hb, tm, 2 * LANES), lambda i, j: (j, i, 0)),
        out_shape=jax.ShapeDtypeStruct((nh, m, 2 * LANES), BF16),
        compiler_params=_params(("parallel", "parallel")),
        name="mm_qk",
    )(x, w, g, c3, sel)


def _mm_plain_kernel(x_ref, w_ref, o_ref):
    o_ref[...] = jnp.dot(x_ref[...], w_ref[...], preferred_element_type=F32).astype(o_ref.dtype)


def mm_plain(x, w, n, tm, tn, out_dtype=BF16):
    m, k = x.shape
    return pl.pallas_call(
        _mm_plain_kernel,
        grid=(pl.cdiv(m, tm), n // tn),
        in_specs=[pl.BlockSpec((tm, k), lambda i, j: (i, 0)),
                  pl.BlockSpec((k, tn), lambda i, j: (0, j))],
        out_specs=pl.BlockSpec((tm, tn), lambda i, j: (i, j)),
        out_shape=jax.ShapeDtypeStruct((m, n), out_dtype),
        compiler_params=_params(("parallel", "parallel")),
        name="mm_plain",
    )(x, w)


def _mm_res_kernel(x_ref, w_ref, r_ref, o_ref, acc_ref, *, tm, nk, first_valid, mask_rows):
    kk = pl.program_id(2)
    part = jnp.dot(x_ref[...], w_ref[...], preferred_element_type=F32)

    def finish(total):
        if mask_rows:
            row = pl.program_id(0) * tm + lax.broadcasted_iota(jnp.int32, total.shape, 0)
            keep = jnp.logical_or(row < SEQ, row >= first_valid)
            total = jnp.where(keep, total, 0.0)
        o_ref[...] = r_ref[...] + total

    if nk == 1:
        finish(part)
    else:
        @pl.when(kk == 0)
        def _():
            acc_ref[...] = part

        @pl.when(jnp.logical_and(kk > 0, kk < nk - 1))
        def _():
            acc_ref[...] += part

        @pl.when(kk == nk - 1)
        def _():
            finish(acc_ref[...] + part)


def mm_residual(x, w, res, m, tm, tn, tk, mask_rows):
    k = x.shape[1]
    n = w.shape[1]
    nk = k // tk
    return pl.pallas_call(
        functools.partial(_mm_res_kernel, tm=tm, nk=nk, first_valid=SEQ + FRONT_PAD,
                          mask_rows=mask_rows),
        grid=(pl.cdiv(m, tm), n // tn, nk),
        in_specs=[pl.BlockSpec((tm, tk), lambda i, j, kk: (i, kk)),
                  pl.BlockSpec((tk, tn), lambda i, j, kk: (kk, j)),
                  pl.BlockSpec((tm, tn), lambda i, j, kk: (i, j))],
        out_specs=pl.BlockSpec((tm, tn), lambda i, j, kk: (i, j)),
        out_shape=jax.ShapeDtypeStruct((m, n), F32),
        scratch_shapes=[pltpu.VMEM((tm, tn), F32)],
        compiler_params=_params(("parallel", "parallel", "arbitrary")),
        name="mm_residual",
    )(x, w, res)


_NT = (((1,), (1,)), ((), ()))


def _gated_store(o_ref, gate_ref, num, den):
    den = jnp.where(den > 0.0, den, 1.0)
    o_ref[...] = (num / den * _silu(gate_ref[0].astype(F32))).astype(o_ref.dtype)


def _attn_bounded_kernel(q_ref, k_ref, v_ref, gate_ref, o_ref, acc_ref, p_ref, *, seq, bq, bk, nq):
    i = pl.program_id(1)
    sub = bq // bk

    def probs(kc, mask):
        t = lax.dot_general(q_ref[0], kc, _NT, preferred_element_type=F32)
        if mask is not None:
            t = jnp.where(mask, t, NEG)
        return jnp.exp2(t).astype(BF16)

    def chunk_probs(n, mask):
        return probs(k_ref[0, pl.ds(pl.multiple_of(n * bk, bk), bk), :], mask)

    def chunk_pv(p, n):
        return jnp.dot(p, v_ref[0, pl.ds(pl.multiple_of(n * bk, bk), bk), :],
                       preferred_element_type=F32)

    def causal(n):
        col_minus_row = (lax.broadcasted_iota(jnp.int32, (bq, bk), 1)
                         - lax.broadcasted_iota(jnp.int32, (bq, bk), 0))
        return col_minus_row <= i * bq - n * bk

    def span(first, count):
        p = p_ref[...]
        tot = None
        for u in range(count):
            nxt = first + u + 1
            p_next = chunk_probs(nxt, causal(nxt) if u == count - 1 else None)
            r = chunk_pv(p, first + u)
            tot = r if tot is None else tot + r
            p = p_next
        p_ref[...] = p
        acc_ref[...] += tot

    kp = k_ref[0, seq:seq + BLOCK, :]
    vp = v_ref[0, seq:seq + BLOCK, :]

    def finish():
        acc = acc_ref[...]
        _gated_store(o_ref, gate_ref, acc[:, 0:LANES], acc[:, LANES:LANES + 1])

    @pl.when(i < nq)
    def _():
        acc_ref[...] = jnp.dot(probs(kp, None), vp, preferred_element_type=F32)
        p_ref[...] = chunk_probs(0, causal(0))

        def body(j, carry):
            span(j * (2 * sub), 2 * sub)
            return carry

        lax.fori_loop(0, i // 2, body, 0)

        @pl.when(i % 2 == 1)
        def _():
            span((i - 1) * sub, sub)

        row = lax.broadcasted_iota(jnp.int32, (bq, bk), 0)
        col = lax.broadcasted_iota(jnp.int32, (bq, bk), 1)
        p = p_ref[...]
        tot = None
        for jj in range(sub):
            if jj + 1 < sub:
                p_next = chunk_probs(i * sub + jj + 1, col + (jj + 1) * bk <= row)
            r = chunk_pv(p, i * sub + jj)
            tot = r if tot is None else tot + r
            p = p_next
        acc_ref[...] += tot
        finish()

    @pl.when(i == nq)
    def _():
        tri = (lax.broadcasted_iota(jnp.int32, (bq, BLOCK), 1)
               <= lax.broadcasted_iota(jnp.int32, (bq, BLOCK), 0))
        acc_ref[...] = jnp.dot(probs(kp, tri), vp, preferred_element_type=F32)
        finish()


def _attn_online_kernel(q_ref, k_ref, v_ref, gate_ref, o_ref, *, seq, bq, nq):
    i = pl.program_id(1)
    q = q_ref[0]

    def update(carry, kc, vc, mask):
        m, l, acc = carry
        s = lax.dot_general(q, kc, _NT, preferred_element_type=F32)
        if mask is not None:
            s = jnp.where(mask, s, NEG)
        m_new = jnp.maximum(m, jnp.max(s, axis=-1, keepdims=True))
        alpha = jnp.exp2(m - m_new)
        p = jnp.exp2(s - m_new)
        l = alpha * l + jnp.sum(p, axis=-1, keepdims=True)
        acc = alpha * acc + jnp.dot(p.astype(BF16), vc, preferred_element_type=F32)
        return m_new, l, acc

    def finish(carry):
        _, l, acc = carry
        _gated_store(o_ref, gate_ref, acc, l)

    init = (jnp.full((bq, 1), NEG, F32), jnp.zeros((bq, 1), F32),
            jnp.zeros((bq, FOX_HEAD_DIM), F32))
    kp = k_ref[0, seq:seq + BLOCK, :]
    vp = v_ref[0, seq:seq + BLOCK, 0:LANES]

    @pl.when(i < nq)
    def _():
        carry = update(init, kp, vp, None)

        def body(j, c):
            off = pl.multiple_of(j * bq, bq)
            return update(c, k_ref[0, pl.ds(off, bq), :], v_ref[0, pl.ds(off, bq), 0:LANES], None)

        carry = lax.fori_loop(0, i, body, carry)
        off = pl.multiple_of(i * bq, bq)
        tri = (lax.broadcasted_iota(jnp.int32, (bq, bq), 1)
               <= lax.broadcasted_iota(jnp.int32, (bq, bq), 0))
        carry = update(carry, k_ref[0, pl.ds(off, bq), :], v_ref[0, pl.ds(off, bq), 0:LANES], tri)
        finish(carry)

    @pl.when(i == nq)
    def _():
        tri = (lax.broadcasted_iota(jnp.int32, (bq, BLOCK), 1)
               <= lax.broadcasted_iota(jnp.int32, (bq, BLOCK), 0))
        finish(update(init, kp, vp, tri))


def fox_attention(q, k, v, gate, seq, bounded, bq, bk=256):
    nh, p, _ = q.shape
    nq = seq // bq
    if bounded:
        body = functools.partial(_attn_bounded_kernel, seq=seq, bq=bq, bk=bk, nq=nq)
        scratch = [pltpu.VMEM((bq, 2 * LANES), F32), pltpu.VMEM((bq, bk), BF16)]
    else:
        body = functools.partial(_attn_online_kernel, seq=seq, bq=bq, nq=nq)
        scratch = []
    return pl.pallas_call(
        body,
        grid=(nh, nq + 1),
        in_specs=[pl.BlockSpec((1, bq, 2 * LANES), lambda h, i: (h, i, 0)),
                  pl.BlockSpec((1, p, 2 * LANES), lambda h, i: (h, 0, 0)),
                  pl.BlockSpec((1, p, 2 * LANES), lambda h, i: (h, 0, 0)),
                  pl.BlockSpec((1, bq, LANES), lambda h, i: (h, i, 0))],
        out_specs=pl.BlockSpec((bq, LANES), lambda h, i: (i, h)),
        out_shape=jax.ShapeDtypeStruct((p, nh * LANES), BF16),
        scratch_shapes=scratch,
        compiler_params=_params(("parallel", "arbitrary")),
        name="fox_attention_bounded" if bounded else "fox_attention_online",
    )(q, k, v, gate)


def _conv_silu(raw_ref, prev_ref, shift_ref, w_ref, b_ref, valid):
    raw = raw_ref[...]
    u = jnp.concatenate([prev_ref[...], raw], axis=0)
    sh = jnp.dot(shift_ref[...], u, preferred_element_type=F32)
    prev_ref[...] = raw
    acc = b_ref[...] + w_ref[CONV_WIDTH - 1:CONV_WIDTH, :] * raw.astype(F32)
    for s in range(1, CONV_WIDTH):
        acc = acc + w_ref[CONV_WIDTH - 1 - s:CONV_WIDTH - s, :] * sh[(s - 1) * BLOCK:s * BLOCK, :]
    return jnp.where(valid, _silu(acc), 0.0)


def _ssd_kernel(z_ref, x_ref, b_ref, c_ref, dtt_ref, alog_ref, shift_ref, wx_ref, wb_ref, wc_ref,
                bx_ref, bb_ref, bc_ref, dsk_ref, ng_ref, y_ref,
                st_ref, y_scr, px_ref, pb_ref, pc_ref):
    c = pl.program_id(1)

    @pl.when(c == 0)
    def _():
        st_ref[...] = jnp.zeros_like(st_ref)
        px_ref[...] = jnp.zeros_like(px_ref)
        pb_ref[...] = jnp.zeros_like(pb_ref)
        pc_ref[...] = jnp.zeros_like(pc_ref)

    row1 = lax.broadcasted_iota(jnp.int32, (BLOCK, 1), 0)
    valid = jnp.logical_or(c > 0, row1 >= FRONT_PAD)
    xs = _conv_silu(x_ref, px_ref, shift_ref, wx_ref, bx_ref, valid)
    bm = _conv_silu(b_ref, pb_ref, shift_ref, wb_ref, bb_ref, valid)
    cm = _conv_silu(c_ref, pc_ref, shift_ref, wc_ref, bc_ref, valid)
    xs_b = xs.astype(BF16)
    bm_b = bm.astype(BF16)
    cm_b = cm.astype(BF16)

    dtt = dtt_ref[...]
    dat = dtt * (-jnp.exp(alog_ref[...]) * LOG2E)
    upper = (lax.broadcasted_iota(jnp.int32, (BLOCK, BLOCK), 0)
             <= lax.broadcasted_iota(jnp.int32, (BLOCK, BLOCK), 1)).astype(BF16)
    hi, mid, lo = _split3(dat)
    cumt = (jnp.dot(hi, upper, preferred_element_type=F32)
            + jnp.dot(mid, upper, preferred_element_type=F32)
            + jnp.dot(lo, upper, preferred_element_type=F32))
    last = cumt[:, BLOCK - 1:BLOCK]
    wdt = jnp.exp2(last - cumt) * dtt

    causal = (lax.broadcasted_iota(jnp.int32, (BLOCK, BLOCK), 1)
              <= lax.broadcasted_iota(jnp.int32, (BLOCK, BLOCK), 0))
    cb = lax.dot_general(cm_b, bm_b, (((1,), (1,)), ((), ())), preferred_element_type=F32)
    cb = jnp.where(causal, cb, 0.0)
    bmt_f = bm.T
    lane = lax.broadcasted_iota(jnp.int32, (BLOCK, LANES), 1)
    left = lane < SSM_HEAD_DIM

    for pr in range(SSM_HPG // 2):
        lhs = []
        cs = []
        elast = []
        for hh in (2 * pr, 2 * pr + 1):
            rh = jnp.broadcast_to(cumt[hh:hh + 1, :], (BLOCK, BLOCK))
            ch = rh.T
            decay = jnp.exp2(jnp.minimum(ch - rh, 0.0))
            lhs.append((cb * decay * dtt[hh:hh + 1, :]).astype(BF16))
            lhs.append((bmt_f * wdt[hh:hh + 1, :]).astype(BF16))
            ech = jnp.exp2(ch)
            cs.append((cm * ech).astype(BF16))
            elast.append(ech[BLOCK - 1:BLOCK, :])
        cols = slice(pr * LANES, (pr + 1) * LANES)
        st_old = st_ref[:, cols]
        r1 = jnp.dot(jnp.concatenate(lhs, axis=0), xs_b[:, cols], preferred_element_type=F32)
        r2 = jnp.dot(jnp.concatenate(cs, axis=0), st_old.astype(BF16),
                     preferred_element_type=F32)
        y_scr[:, cols] = jnp.where(left, r1[0:BLOCK] + r2[0:BLOCK],
                                   r1[2 * BLOCK:3 * BLOCK] + r2[BLOCK:2 * BLOCK])
        ds = jnp.where(left, r1[BLOCK:2 * BLOCK], r1[3 * BLOCK:4 * BLOCK])
        el = jnp.where(left[0:1, :], elast[0], elast[1])
        st_ref[:, cols] = st_old * el + ds

    z = z_ref[...].astype(F32)
    y = y_scr[...] + dsk_ref[...] * xs
    y = y * _silu(z)
    y = y * lax.rsqrt(jnp.mean(y * y, axis=-1, keepdims=True) + EPS)
    y_ref[...] = (y * ng_ref[...]).astype(y_ref.dtype)


def ssd_scan(zx, dtt, alog_col, conv_w, conv_b, dsk_rep, norm_g):
    p = zx.shape[0]
    nch = p // BLOCK
    gw = GROUP_WIDTH
    zb = SSM_INNER // gw
    bb = 2 * SSM_INNER // SSM_STATE
    cb = bb + SSM_GROUPS
    cwx = 0
    cwb = SSM_INNER // SSM_STATE
    cwc = cwb + SSM_GROUPS
    rows = lambda c: (c + nch - 1) % nch
    t = jnp.arange(BLOCK)[None, :, None]
    s = jnp.arange(1, CONV_WIDTH)[:, None, None]
    shift = (jnp.arange(2 * BLOCK)[None, None, :] == BLOCK + t - s).astype(BF16)
    shift = shift.reshape((CONV_WIDTH - 1) * BLOCK, 2 * BLOCK)
    return pl.pallas_call(
        _ssd_kernel,
        grid=(SSM_GROUPS, nch),
        in_specs=[
            pl.BlockSpec((BLOCK, gw), lambda g, c: (rows(c), g)),
            pl.BlockSpec((BLOCK, gw), lambda g, c: (rows(c), zb + g)),
            pl.BlockSpec((BLOCK, SSM_STATE), lambda g, c: (rows(c), bb + g)),
            pl.BlockSpec((BLOCK, SSM_STATE), lambda g, c: (rows(c), cb + g)),
            pl.BlockSpec((SSM_HPG, BLOCK), lambda g, c: (g, rows(c))),
            pl.BlockSpec((SSM_HPG, 1), lambda g, c: (g, 0)),
            pl.BlockSpec(((CONV_WIDTH - 1) * BLOCK, 2 * BLOCK), lambda g, c: (0, 0)),
            pl.BlockSpec((CONV_WIDTH, gw), lambda g, c: (0, cwx + g)),
            pl.BlockSpec((CONV_WIDTH, SSM_STATE), lambda g, c: (0, cwb + g)),
            pl.BlockSpec((CONV_WIDTH, SSM_STATE), lambda g, c: (0, cwc + g)),
            pl.BlockSpec((1, gw), lambda g, c: (0, cwx + g)),
            pl.BlockSpec((1, SSM_STATE), lambda g, c: (0, cwb + g)),
            pl.BlockSpec((1, SSM_STATE), lambda g, c: (0, cwc + g)),
            pl.BlockSpec((1, gw), lambda g, c: (0, g)),
            pl.BlockSpec((1, gw), lambda g, c: (0, g)),
        ],
        out_specs=pl.BlockSpec((BLOCK, gw), lambda g, c: (rows(c), g)),
        out_shape=jax.ShapeDtypeStruct((p, SSM_INNER), BF16),
        scratch_shapes=[pltpu.VMEM((SSM_STATE, gw), F32),
                        pltpu.VMEM((BLOCK, gw), F32),
                        pltpu.VMEM((BLOCK, gw), BF16),
                        pltpu.VMEM((BLOCK, SSM_STATE), BF16),
                        pltpu.VMEM((BLOCK, SSM_STATE), BF16)],
        compiler_params=_params(("parallel", "arbitrary")),
        name="ssd_scan",
    )(zx, zx, zx, zx, dtt, alog_col, shift, conv_w, conv_w, conv_w, conv_b, conv_b, conv_b,
      dsk_rep, norm_g)


def _dt_kernel(raw_ref, b_ref, o_ref):
    o_ref[...] = jax.nn.softplus(raw_ref[...] + b_ref[...]).T


def dt_transpose(raw, bias, tr=512):
    p = raw.shape[0]
    return pl.pallas_call(
        _dt_kernel,
        grid=(pl.cdiv(p, tr),),
        in_specs=[pl.BlockSpec((tr, LANES), lambda i: (i, 0)),
                  pl.BlockSpec((1, LANES), lambda i: (0, 0))],
        out_specs=pl.BlockSpec((LANES, tr), lambda i: (0, i)),
        out_shape=jax.ShapeDtypeStruct((LANES, p), F32),
        compiler_params=_params(("parallel",)),
        name="dt_transpose",
    )(raw, bias)


def _bias_selectors():
    h = jnp.arange(FOX_HEADS)
    r = jnp.arange(LANES)[None, :, None]
    c = jnp.arange(LANES)[None, None, :]
    hh = h[:, None, None]
    own = (r == hh + 32 * c) & (c < 3)
    ones_q = (r == 96) & (c >= 3) & (c < 6)
    sel_q = (own | ones_q).astype(BF16)
    own_k = (r == hh + 32 * (c - 3)) & (c >= 3) & (c < 6)
    ones_k = (r == 96) & (c < 3)
    sel_k = (own_k | ones_k).astype(BF16)
    return sel_q, sel_k


def kernel(x, meta, norm_g, fox_w_in, fox_b_f, fox_q_g, fox_k_g, fox_w_out, ssm_w_in, ssm_conv_w,
           ssm_conv_b, ssm_dt_bias, ssm_a_log, ssm_d, ssm_norm_g, ssm_w_out):
    p = SEQ + BLOCK
    h0 = jnp.concatenate([x[0], jnp.zeros((FRONT_PAD, D_MODEL), x.dtype),
                          meta.astype(x.dtype)], axis=0)

    w_in = fox_w_in[0].astype(BF16)
    w_f = fox_w_in[0][:, 4 * FOX_WIDTH:]
    w_f = jnp.concatenate([w_f, w_f, w_f, jnp.zeros_like(w_f)], axis=1).astype(BF16)
    b_f = fox_b_f[0]
    b_f = jnp.concatenate([b_f, b_f, b_f, jnp.zeros_like(b_f)])[None, :]
    hn, f_side = norm_side(h0, norm_g[0][None, :], w_f)
    bound = (1.01 * FOX_HEAD_DIM ** 0.5) * jnp.max(jnp.abs(fox_q_g[0])) * jnp.max(jnp.abs(fox_k_g[0]))
    shift = bound * LOG2E
    cq, ck = fcum(f_side, b_f, jnp.full((1, LANES), shift, F32))
    sel_q, sel_k = _bias_selectors()
    tm, tn = 1376, 512
    q = mm_qk(hn, w_in, 0, fox_q_g[0][None, :], cq, sel_q, FOX_HEAD_DIM ** -0.5 * LOG2E, tm, tn)
    k = mm_qk(hn, w_in, FOX_WIDTH, fox_k_g[0][None, :], ck, sel_k, 1.0, tm, tn)
    v = mm_heads(hn, w_in, 2 * FOX_WIDTH, FOX_WIDTH, tm, tn, ones_pad=True)
    gate = mm_heads(hn, w_in, 3 * FOX_WIDTH, FOX_WIDTH, tm, tn)
    og = lax.cond(2.0 * shift <= SAFE_LOG2_RANGE,
                  functools.partial(fox_attention, seq=SEQ, bounded=True, bq=1024),
                  functools.partial(fox_attention, seq=SEQ, bounded=False, bq=512),
                  q, k, v, gate)
    h1 = mm_residual(og, fox_w_out[0].astype(BF16), h0, p, tm, tn, FOX_WIDTH, True)

    w_in = ssm_w_in[0]
    n_zx = 2 * SSM_INNER + 2 * SSM_GROUPS * SSM_STATE
    hn, dt_raw = norm_side(h1, norm_g[1][None, :], w_in[:, n_zx:].astype(BF16))
    dtt = dt_transpose(dt_raw, ssm_dt_bias[0][None, :])
    zx = mm_plain(hn, w_in.astype(BF16), n_zx, tm, tn)
    dsk_rep = jnp.repeat(ssm_d[0], SSM_HEAD_DIM)[None, :]
    y = ssd_scan(zx, dtt, ssm_a_log[0][:, None], ssm_conv_w[0], ssm_conv_b[0][None, :],
                 dsk_rep, ssm_norm_g[0][None, :])
    out = mm_residual(y, ssm_w_out[0].astype(BF16), h1, SEQ, 1024, 512, 4096, False)
    return out[None]
```

```python
import functools

import jax
import jax.numpy as jnp
from jax import lax
from jax.experimental import pallas as pl
from jax.experimental.pallas import tpu as pltpu

D_MODEL = 4096
SEQ = 16384
N_META = 16
BLOCK = 128
FRONT_PAD = BLOCK - N_META
EPS = 1e-6
NEG = -1e30

FOX_HEADS = 32
FOX_HEAD_DIM = 128
FOX_WIDTH = FOX_HEADS * FOX_HEAD_DIM

SSM_INNER = 8192
SSM_HEAD_DIM = 64
SSM_HEADS = SSM_INNER // SSM_HEAD_DIM
SSM_GROUPS = 8
SSM_HPG = SSM_HEADS // SSM_GROUPS
SSM_STATE = 128
CONV_WIDTH = 4
GROUP_WIDTH = SSM_HPG * SSM_HEAD_DIM

LOG2E = 1.4426950408889634
SAFE_LOG2_RANGE = 96.0

LANES = 128
VMEM_LIMIT = 56 * 1024 * 1024

F32 = jnp.float32
BF16 = jnp.bfloat16


def _params(sem, vmem=VMEM_LIMIT):
    return pltpu.CompilerParams(dimension_semantics=sem, vmem_limit_bytes=vmem)


def _silu(x):
    h = 0.5 * x
    return h * (jnp.tanh(h) + 1.0)


def _split3(x):
    hi = x.astype(BF16)
    r1 = x - hi.astype(F32)
    mid = r1.astype(BF16)
    lo = (r1 - mid.astype(F32)).astype(BF16)
    return hi, mid, lo


def _norm_side_kernel(h_ref, g_ref, w_ref, *rest):
    hn_ref, side_ref = rest[-2:]
    x = h_ref[...]
    ms = jnp.mean(x * x, axis=-1, keepdims=True)
    hn = (x * lax.rsqrt(ms + EPS) * g_ref[...]).astype(BF16)
    hn_ref[...] = hn
    side_ref[...] = jnp.dot(hn, w_ref[...], preferred_element_type=F32)


def norm_side(h, g, w_side, rows, tr=512, into=None, row0=0):
    n, d = h.shape[-2:]
    tr = min(tr, n)
    b0 = row0 // tr
    if h.ndim == 3:
        h_spec = pl.BlockSpec((None, tr, d), lambda i: (0, i, 0))
    else:
        h_spec = pl.BlockSpec((tr, d), lambda i: (i, 0))
    in_specs = [h_spec,
                pl.BlockSpec((1, d), lambda i: (0, 0)),
                pl.BlockSpec((d, LANES), lambda i: (0, 0))]
    args = [h, g, w_side]
    aliases = {}
    if into is not None:
        in_specs += [pl.BlockSpec(memory_space=pl.ANY)] * 2
        args += list(into)
        aliases = {3: 0, 4: 1}
    return pl.pallas_call(
        _norm_side_kernel,
        grid=(pl.cdiv(n, tr),),
        in_specs=in_specs,
        out_specs=[pl.BlockSpec((tr, d), lambda i: (b0 + i, 0)),
                   pl.BlockSpec((tr, LANES), lambda i: (b0 + i, 0))],
        out_shape=[jax.ShapeDtypeStruct((rows, d), BF16),
                   jax.ShapeDtypeStruct((rows, LANES), F32)],
        input_output_aliases=aliases,
        compiler_params=_params(("parallel",)),
        name="norm_side",
    )(*args)


def _fcum_kernel(side_ref, b_ref, shift_ref, tri_ref, cq_ref, ck_ref, carry_ref):
    i = pl.program_id(0)

    @pl.when(i == 0)
    def _():
        carry_ref[...] = jnp.zeros_like(carry_ref)

    row = lax.broadcasted_iota(jnp.int32, (BLOCK, LANES), 0)
    lane = lax.broadcasted_iota(jnp.int32, (BLOCK, LANES), 1)
    valid = jnp.logical_or(i > 0, row >= FRONT_PAD)
    lf = jax.nn.log_sigmoid(side_ref[...] + b_ref[...])
    lf = jnp.where(valid, lf, 0.0)
    hi, mid, lo = _split3(lf)
    tri = tri_ref[...]
    cum = (jnp.dot(tri, hi, preferred_element_type=F32)
           + jnp.dot(tri, mid, preferred_element_type=F32)
           + jnp.dot(tri, lo, preferred_element_type=F32)) + carry_ref[0:1, :]
    carry_ref[...] = jnp.broadcast_to(cum[BLOCK - 1:BLOCK, :], carry_ref.shape)

    def triple(v):
        vh, vm, vl = _split3(v)
        one = jnp.ones_like(vh)
        return jnp.where(lane < 32, vh, jnp.where(lane < 64, vm, jnp.where(lane < 96, vl, one)))

    cum2 = cum * LOG2E
    cq_ref[...] = triple(cum2 - shift_ref[...])
    ck_ref[...] = triple(jnp.where(valid, -cum2, NEG))


def fcum(side, b_rep, shift):
    p = side.shape[0]
    nblk = p // BLOCK
    tri = jnp.tril(jnp.ones((BLOCK, BLOCK), BF16))
    blk = lambda i: ((i + nblk - 1) % nblk, 0)
    return pl.pallas_call(
        _fcum_kernel,
        grid=(nblk,),
        in_specs=[pl.BlockSpec((BLOCK, LANES), blk),
                  pl.BlockSpec((1, LANES), lambda i: (0, 0)),
                  pl.BlockSpec((1, LANES), lambda i: (0, 0)),
                  pl.BlockSpec((BLOCK, BLOCK), lambda i: (0, 0))],
        out_specs=[pl.BlockSpec((BLOCK, LANES), blk),
                   pl.BlockSpec((BLOCK, LANES), blk)],
        out_shape=[jax.ShapeDtypeStruct((p, LANES), BF16),
                   jax.ShapeDtypeStruct((p, LANES), BF16)],
        scratch_shapes=[pltpu.VMEM((8, LANES), F32)],
        compiler_params=_params(("arbitrary",)),
        name="fcum",
    )(side, b_rep, shift, tri)


def _mm_heads_kernel(x_ref, w_ref, o_ref, *, ones_pad):
    acc = jnp.dot(x_ref[...], w_ref[...], preferred_element_type=F32)
    for j in range(o_ref.shape[0]):
        o_ref[j, :, 0:LANES] = acc[:, j * LANES:(j + 1) * LANES].astype(o_ref.dtype)
        if ones_pad:
            o_ref[j, :, LANES:2 * LANES] = jnp.ones((o_ref.shape[1], LANES), o_ref.dtype)


def mm_heads(x, w, col0, n, tm, tn, ones_pad=False):
    m, k = x.shape
    c0 = col0 // tn
    hb = tn // LANES
    wo = 2 * LANES if ones_pad else LANES
    return pl.pallas_call(
        functools.partial(_mm_heads_kernel, ones_pad=ones_pad),
        grid=(pl.cdiv(m, tm), n // tn),
        in_specs=[pl.BlockSpec((tm, k), lambda i, j: (i, 0)),
                  pl.BlockSpec((k, tn), lambda i, j: (0, c0 + j))],
        out_specs=pl.BlockSpec((hb, tm, wo), lambda i, j: (j, i, 0)),
        out_shape=jax.ShapeDtypeStruct((n // LANES, m, wo), BF16),
        compiler_params=_params(("parallel", "parallel")),
        name="mm_heads",
    )(x, w)


def _mm_qk_kernel(x_ref, w_ref, g_ref, c3_ref, sel_ref, o_ref, *, scale):
    acc = jnp.dot(x_ref[...], w_ref[...], preferred_element_type=F32)
    c3 = c3_ref[...]
    g = g_ref[...] * scale
    for j in range(o_ref.shape[0]):
        a = acc[:, j * LANES:(j + 1) * LANES]
        ms = jnp.mean(a * a, axis=-1, keepdims=True)
        o_ref[j, :, 0:LANES] = (a * lax.rsqrt(ms + EPS) * g).astype(BF16)
        o_ref[j, :, LANES:2 * LANES] = jnp.dot(
            c3, sel_ref[j], preferred_element_type=F32).astype(BF16)


def mm_qk(x, w, col0, g, c3, sel, scale, tm, tn):
    m, k = x.shape
    c0 = col0 // tn
    hb = tn // LANES
    nh = sel.shape[0]
    return pl.pallas_call(
        functools.partial(_mm_qk_kernel, scale=scale),
        grid=(pl.cdiv(m, tm), nh // hb),
        in_specs=[pl.BlockSpec((tm, k), lambda i, j: (i, 0)),
                  pl.BlockSpec((k, tn), lambda i, j: (0, c0 + j)),
                  pl.BlockSpec((1, LANES), lambda i, j: (0, 0)),
                  pl.BlockSpec((tm, LANES), lambda i, j: (i, 0)),
                  pl.BlockSpec((hb, LANES, LANES), lambda i, j: (j, 0, 0))],
        out_specs=pl.BlockSpec((hb, tm, 2 * LANES), lambda i, j: (j, i, 0)),
        out_shape=jax.ShapeDtypeStruct((nh, m, 2 * LANES), BF16),
        compiler_params=_params(("parallel", "parallel")),
        name="mm_qk",
    )(x, w, g, c3, sel)


def _mm_plain_kernel(x_ref, w_ref, o_ref):
    o_ref[...] = jnp.dot(x_ref[...], w_ref[...], preferred_element_type=F32).astype(o_ref.dtype)


def mm_plain(x, w, n, tm, tn, out_dtype=BF16):
    m, k = x.shape
    return pl.pallas_call(
        _mm_plain_kernel,
        grid=(pl.cdiv(m, tm), n // tn),
        in_specs=[pl.BlockSpec((tm, k), lambda i, j: (i, 0)),
                  pl.BlockSpec((k, tn), lambda i, j: (0, j))],
        out_specs=pl.BlockSpec((tm, tn), lambda i, j: (i, j)),
        out_shape=jax.ShapeDtypeStruct((m, n), out_dtype),
        compiler_params=_params(("parallel", "parallel")),
        name="mm_plain",
    )(x, w)


def _mm_res_kernel(x_ref, w_ref, r_ref, *rest, nk, first_valid):
    o_ref, acc_ref = rest[-2:]
    kk = pl.program_id(2)
    part = jnp.dot(x_ref[...], w_ref[...], preferred_element_type=F32)

    def finish(total):
        if first_valid:
            row = lax.broadcasted_iota(jnp.int32, total.shape, 0)
            total = jnp.where(row >= first_valid, total, 0.0)
        o_ref[...] = r_ref[...] + total

    if nk == 1:
        finish(part)
    else:
        @pl.when(kk == 0)
        def _():
            acc_ref[...] = part

        @pl.when(jnp.logical_and(kk > 0, kk < nk - 1))
        def _():
            acc_ref[...] += part

        @pl.when(kk == nk - 1)
        def _():
            finish(acc_ref[...] + part)


def mm_residual(x, w, res, m, tm, tn, tk, *, out_rows=None, row0=0, into=None, first_valid=0):
    k = x.shape[1]
    n = w.shape[1]
    nk = k // tk
    b0 = row0 // tm
    if res.ndim == 3:
        res_spec = pl.BlockSpec((None, tm, tn), lambda i, j, kk: (0, i, j))
    else:
        res_spec = pl.BlockSpec((tm, tn), lambda i, j, kk: (i, j))
    in_specs = [pl.BlockSpec((tm, tk), lambda i, j, kk: (b0 + i, kk)),
                pl.BlockSpec((tk, tn), lambda i, j, kk: (kk, j)),
                res_spec]
    args = [x, w, res]
    aliases = {}
    if into is not None:
        in_specs.append(pl.BlockSpec(memory_space=pl.ANY))
        args.append(into)
        aliases = {3: 0}
    if out_rows is None:
        out_spec = pl.BlockSpec((None, tm, tn), lambda i, j, kk: (0, i, j))
        out_shape = jax.ShapeDtypeStruct((1, m, n), F32)
    else:
        out_spec = pl.BlockSpec((tm, tn), lambda i, j, kk: (b0 + i, j))
        out_shape = jax.ShapeDtypeStruct((out_rows, n), F32)
    return pl.pallas_call(
        functools.partial(_mm_res_kernel, nk=nk, first_valid=first_valid),
        grid=(m // tm, n // tn, nk),
        in_specs=in_specs,
        out_specs=out_spec,
        out_shape=out_shape,
        input_output_aliases=aliases,
        scratch_shapes=[pltpu.VMEM((tm, tn), F32)],
        compiler_params=_params(("parallel", "parallel", "arbitrary")),
        name="mm_residual",
    )(*args)


_NT = (((1,), (1,)), ((), ()))


def _gated_store(o_ref, gate_ref, num, den):
    den = jnp.where(den > 0.0, den, 1.0)
    o_ref[...] = (num / den * _silu(gate_ref[0].astype(F32))).astype(o_ref.dtype)


def _attn_bounded_kernel(q_ref, k_ref, v_ref, gate_ref, o_ref, acc_ref, p_ref, *, seq, bq, bk, nq):
    i = pl.program_id(1)
    sub = bq // bk

    def probs(kc, mask):
        t = lax.dot_general(q_ref[0], kc, _NT, preferred_element_type=F32)
        if mask is not None:
            t = jnp.where(mask, t, NEG)
        return jnp.exp2(t).astype(BF16)

    def chunk_probs(n, mask):
        return probs(k_ref[0, pl.ds(pl.multiple_of(n * bk, bk), bk), :], mask)

    def chunk_pv(p, n):
        return jnp.dot(p, v_ref[0, pl.ds(pl.multiple_of(n * bk, bk), bk), :],
                       preferred_element_type=F32)

    def causal(n):
        col_minus_row = (lax.broadcasted_iota(jnp.int32, (bq, bk), 1)
                         - lax.broadcasted_iota(jnp.int32, (bq, bk), 0))
        return col_minus_row <= i * bq - n * bk

    def span(first, count):
        p = p_ref[...]
        tot = None
        for u in range(count):
            nxt = first + u + 1
            p_next = chunk_probs(nxt, causal(nxt) if u == count - 1 else None)
            r = chunk_pv(p, first + u)
            tot = r if tot is None else tot + r
            p = p_next
        p_ref[...] = p
        acc_ref[...] += tot

    kp = k_ref[0, seq:seq + BLOCK, :]
    vp = v_ref[0, seq:seq + BLOCK, :]

    def finish():
        acc = acc_ref[...]
        _gated_store(o_ref, gate_ref, acc[:, 0:LANES], acc[:, LANES:LANES + 1])

    @pl.when(i < nq)
    def _():
        acc_ref[...] = jnp.zeros_like(acc_ref)
        p_ref[...] = chunk_probs(0, causal(0))

        def body(j, carry):
            span(j * (2 * sub), 2 * sub)
            return carry

        lax.fori_loop(0, i // 2, body, 0)

        @pl.when(i % 2 == 1)
        def _():
            span((i - 1) * sub, sub)

        tot = jnp.dot(probs(kp, None), vp, preferred_element_type=F32)
        tot = tot + chunk_pv(p_ref[...], i * sub)
        tri = (lax.broadcasted_iota(jnp.int32, (bk, bk), 1)
               <= lax.broadcasted_iota(jnp.int32, (bk, bk), 0))
        parts = []
        for jj in range(1, sub):
            n = i * sub + jj
            kc = k_ref[0, pl.ds(pl.multiple_of(n * bk, bk), bk), :]
            t = lax.dot_general(q_ref[0, jj * bk:bq, :], kc, _NT, preferred_element_type=F32)
            top = jnp.where(tri, t[0:bk], NEG)
            t = top if jj == sub - 1 else jnp.concatenate([top, t[bk:]], axis=0)
            parts.append(chunk_pv(jnp.exp2(t).astype(BF16), n))
        acc_ref[...] += tot
        for jj in range(1, sub):
            acc_ref[jj * bk:bq, :] += parts[jj - 1]
        finish()

    @pl.when(i == nq)
    def _():
        tri = (lax.broadcasted_iota(jnp.int32, (bq, BLOCK), 1)
               <= lax.broadcasted_iota(jnp.int32, (bq, BLOCK), 0))
        acc_ref[...] = jnp.dot(probs(kp, tri), vp, preferred_element_type=F32)
        finish()


def _attn_online_kernel(q_ref, k_ref, v_ref, gate_ref, o_ref, *, seq, bq, nq):
    i = pl.program_id(1)
    q = q_ref[0]

    def update(carry, kc, vc, mask):
        m, l, acc = carry
        s = lax.dot_general(q, kc, _NT, preferred_element_type=F32)
        if mask is not None:
            s = jnp.where(mask, s, NEG)
        m_new = jnp.maximum(m, jnp.max(s, axis=-1, keepdims=True))
        alpha = jnp.exp2(m - m_new)
        p = jnp.exp2(s - m_new)
        l = alpha * l + jnp.sum(p, axis=-1, keepdims=True)
        acc = alpha * acc + jnp.dot(p.astype(BF16), vc, preferred_element_type=F32)
        return m_new, l, acc

    def finish(carry):
        _, l, acc = carry
        _gated_store(o_ref, gate_ref, acc, l)

    init = (jnp.full((bq, 1), NEG, F32), jnp.zeros((bq, 1), F32),
            jnp.zeros((bq, FOX_HEAD_DIM), F32))
    kp = k_ref[0, seq:seq + BLOCK, :]
    vp = v_ref[0, seq:seq + BLOCK, 0:LANES]

    @pl.when(i < nq)
    def _():
        carry = update(init, kp, vp, None)

        def body(j, c):
            off = pl.multiple_of(j * bq, bq)
            return update(c, k_ref[0, pl.ds(off, bq), :], v_ref[0, pl.ds(off, bq), 0:LANES], None)

        carry = lax.fori_loop(0, i, body, carry)
        off = pl.multiple_of(i * bq, bq)
        tri = (lax.broadcasted_iota(jnp.int32, (bq, bq), 1)
               <= lax.broadcasted_iota(jnp.int32, (bq, bq), 0))
        carry = update(carry, k_ref[0, pl.ds(off, bq), :], v_ref[0, pl.ds(off, bq), 0:LANES], tri)
        finish(carry)

    @pl.when(i == nq)
    def _():
        tri = (lax.broadcasted_iota(jnp.int32, (bq, BLOCK), 1)
               <= lax.broadcasted_iota(jnp.int32, (bq, BLOCK), 0))
        finish(update(init, kp, vp, tri))


def fox_attention(q, k, v, gate, seq, bounded, bq, bk=256):
    nh, p, _ = q.shape
    nq = seq // bq
    if bounded:
        body = functools.partial(_attn_bounded_kernel, seq=seq, bq=bq, bk=bk, nq=nq)
        scratch = [pltpu.VMEM((bq, 2 * LANES), F32), pltpu.VMEM((bq, bk), BF16)]
    else:
        body = functools.partial(_attn_online_kernel, seq=seq, bq=bq, nq=nq)
        scratch = []
    return pl.pallas_call(
        body,
        grid=(nh, nq + 1),
        in_specs=[pl.BlockSpec((1, bq, 2 * LANES), lambda h, i: (h, i, 0)),
                  pl.BlockSpec((1, p, 2 * LANES), lambda h, i: (h, 0, 0)),
                  pl.BlockSpec((1, p, 2 * LANES), lambda h, i: (h, 0, 0)),
                  pl.BlockSpec((1, bq, LANES), lambda h, i: (h, i, 0))],
        out_specs=pl.BlockSpec((bq, LANES), lambda h, i: (i, h)),
        out_shape=jax.ShapeDtypeStruct((p, nh * LANES), BF16),
        scratch_shapes=scratch,
        compiler_params=_params(("parallel", "arbitrary")),
        name="fox_attention_bounded" if bounded else "fox_attention_online",
    )(q, k, v, gate)


def _conv_silu(raw_ref, prev_ref, slot, shift_ref, w_ref, b_ref, valid):
    raw = raw_ref[...]
    u = jnp.concatenate([prev_ref[1 - slot], raw], axis=0)
    sh = jnp.dot(shift_ref[...], u, preferred_element_type=F32)
    prev_ref[slot] = raw
    acc = b_ref[...] + w_ref[CONV_WIDTH - 1:CONV_WIDTH, :] * raw.astype(F32)
    for s in range(1, CONV_WIDTH):
        acc = acc + w_ref[CONV_WIDTH - 1 - s:CONV_WIDTH - s, :] * sh[(s - 1) * BLOCK:s * BLOCK, :]
    return jnp.where(valid, _silu(acc), 0.0)


def _ssd_kernel(z_ref, x_ref, b_ref, c_ref, dtt_ref, alog_ref, shift_ref, wx_ref, wb_ref, wc_ref,
                bx_ref, bb_ref, bc_ref, dsk_ref, ng_ref, y_ref,
                st_ref, y_scr, px_ref, pb_ref, pc_ref, *, gps):
    c = pl.program_id(1)

    @pl.when(c == 0)
    def _():
        st_ref[...] = jnp.zeros_like(st_ref)
        px_ref[...] = jnp.zeros_like(px_ref)
        pb_ref[...] = jnp.zeros_like(pb_ref)
        pc_ref[...] = jnp.zeros_like(pc_ref)

    row1 = lax.broadcasted_iota(jnp.int32, (BLOCK, 1), 0)
    valid = jnp.logical_or(c > 0, row1 >= FRONT_PAD)
    slot = c % 2
    xs_all = _conv_silu(x_ref, px_ref, slot, shift_ref, wx_ref, bx_ref, valid)
    bm_all = _conv_silu(b_ref, pb_ref, slot, shift_ref, wb_ref, bb_ref, valid)
    cm_all = _conv_silu(c_ref, pc_ref, slot, shift_ref, wc_ref, bc_ref, valid)

    dtt_all = dtt_ref[...]
    dat = dtt_all * (-jnp.exp(alog_ref[...]) * LOG2E)
    upper = (lax.broadcasted_iota(jnp.int32, (BLOCK, BLOCK), 0)
             <= lax.broadcasted_iota(jnp.int32, (BLOCK, BLOCK), 1)).astype(BF16)
    hi, mid, lo = _split3(dat)
    cumt_all = (jnp.dot(hi, upper, preferred_element_type=F32)
                + jnp.dot(mid, upper, preferred_element_type=F32)
                + jnp.dot(lo, upper, preferred_element_type=F32))
    last = cumt_all[:, BLOCK - 1:BLOCK]
    wdt_all = jnp.exp2(last - cumt_all) * dtt_all

    causal = (lax.broadcasted_iota(jnp.int32, (BLOCK, BLOCK), 1)
              <= lax.broadcasted_iota(jnp.int32, (BLOCK, BLOCK), 0))
    lane = lax.broadcasted_iota(jnp.int32, (BLOCK, LANES), 1)
    left = lane < SSM_HEAD_DIM

    for gi in range(gps):
        gcols = slice(gi * GROUP_WIDTH, (gi + 1) * GROUP_WIDTH)
        ncols = slice(gi * SSM_STATE, (gi + 1) * SSM_STATE)
        hrows = slice(gi * SSM_HPG, (gi + 1) * SSM_HPG)
        xs = xs_all[:, gcols]
        bm = bm_all[:, ncols]
        cm = cm_all[:, ncols]
        cumt, wdt = cumt_all[hrows], wdt_all[hrows]
        l2dt = jnp.log2(dtt_all[hrows])
        cumt_dt = cumt - l2dt
        xs_b = xs.astype(BF16)
        cb = lax.dot_general(cm.astype(BF16), bm.astype(BF16), _NT, preferred_element_type=F32)
        cb = jnp.where(causal, cb, 0.0)
        bmt_f = bm.T

        for pr in range(SSM_HPG // 2):
            lhs_y = []
            lhs_s = []
            elast = []
            for hh in (2 * pr, 2 * pr + 1):
                rh = jnp.broadcast_to(cumt[hh:hh + 1, :], (BLOCK, BLOCK))
                ch = rh.T
                decay_dt = jnp.exp2(jnp.minimum(ch - cumt_dt[hh:hh + 1, :], l2dt[hh:hh + 1, :]))
                ech = jnp.exp2(ch)
                lhs_y.append(jnp.concatenate([(cb * decay_dt).astype(BF16),
                                              (cm * ech).astype(BF16)], axis=1))
                lhs_s.append((bmt_f * wdt[hh:hh + 1, :]).astype(BF16))
                elast.append(ech[BLOCK - 1:BLOCK, :])
            cols = slice(gi * GROUP_WIDTH + pr * LANES, gi * GROUP_WIDTH + (pr + 1) * LANES)
            st_old = st_ref[:, cols]
            xs_pair = xs_b[:, pr * LANES:(pr + 1) * LANES]
            ry = jnp.dot(jnp.concatenate(lhs_y, axis=0),
                         jnp.concatenate([xs_pair, st_old.astype(BF16)], axis=0),
                         preferred_element_type=F32)
            rs = jnp.dot(jnp.concatenate(lhs_s, axis=0), xs_pair, preferred_element_type=F32)
            y_scr[:, cols] = jnp.where(left, ry[0:BLOCK], ry[BLOCK:2 * BLOCK])
            ds = jnp.where(left, rs[0:BLOCK], rs[BLOCK:2 * BLOCK])
            el = jnp.where(left[0:1, :], elast[0], elast[1])
            st_ref[:, cols] = st_old * el + ds

        z = z_ref[:, gcols].astype(F32)
        y = y_scr[:, gcols] + dsk_ref[:, gcols] * xs
        y = y * _silu(z)
        y = y * lax.rsqrt(jnp.mean(y * y, axis=-1, keepdims=True) + EPS)
        y_ref[:, gcols] = (y * ng_ref[:, gcols]).astype(y_ref.dtype)


def ssd_scan(zx, dtt, alog_col, conv_w, conv_b, dsk_rep, norm_g, gps=2):
    p = zx.shape[0]
    nch = p // BLOCK
    gw = gps * GROUP_WIDTH
    nw = gps * SSM_STATE
    hw = gps * SSM_HPG
    zb = SSM_INNER // gw
    bb = 2 * SSM_INNER // nw
    cb = bb + SSM_GROUPS // gps
    cwx = 0
    cwb = SSM_INNER // nw
    cwc = cwb + SSM_GROUPS // gps
    rows = lambda c: (c + nch - 1) % nch
    t = jnp.arange(BLOCK)[None, :, None]
    s = jnp.arange(1, CONV_WIDTH)[:, None, None]
    shift = (jnp.arange(2 * BLOCK)[None, None, :] == BLOCK + t - s).astype(BF16)
    shift = shift.reshape((CONV_WIDTH - 1) * BLOCK, 2 * BLOCK)
    return pl.pallas_call(
        functools.partial(_ssd_kernel, gps=gps),
        grid=(SSM_GROUPS // gps, nch),
        in_specs=[
            pl.BlockSpec((BLOCK, gw), lambda g, c: (rows(c), g)),
            pl.BlockSpec((BLOCK, gw), lambda g, c: (rows(c), zb + g)),
            pl.BlockSpec((BLOCK, nw), lambda g, c: (rows(c), bb + g)),
            pl.BlockSpec((BLOCK, nw), lambda g, c: (rows(c), cb + g)),
            pl.BlockSpec((hw, BLOCK), lambda g, c: (g, rows(c))),
            pl.BlockSpec((hw, 1), lambda g, c: (g, 0)),
            pl.BlockSpec(((CONV_WIDTH - 1) * BLOCK, 2 * BLOCK), lambda g, c: (0, 0)),
            pl.BlockSpec((CONV_WIDTH, gw), lambda g, c: (0, cwx + g)),
            pl.BlockSpec((CONV_WIDTH, nw), lambda g, c: (0, cwb + g)),
            pl.BlockSpec((CONV_WIDTH, nw), lambda g, c: (0, cwc + g)),
            pl.BlockSpec((1, gw), lambda g, c: (0, cwx + g)),
            pl.BlockSpec((1, nw), lambda g, c: (0, cwb + g)),
            pl.BlockSpec((1, nw), lambda g, c: (0, cwc + g)),
            pl.BlockSpec((1, gw), lambda g, c: (0, g)),
            pl.BlockSpec((1, gw), lambda g, c: (0, g)),
        ],
        out_specs=pl.BlockSpec((BLOCK, gw), lambda g, c: (rows(c), g)),
        out_shape=jax.ShapeDtypeStruct((p, SSM_INNER), BF16),
        scratch_shapes=[pltpu.VMEM((SSM_STATE, gw), F32),
                        pltpu.VMEM((BLOCK, gw), F32),
                        pltpu.VMEM((2, BLOCK, gw), BF16),
                        pltpu.VMEM((2, BLOCK, nw), BF16),
                        pltpu.VMEM((2, BLOCK, nw), BF16)],
        compiler_params=_params(("parallel", "arbitrary")),
        name="ssd_scan",
    )(zx, zx, zx, zx, dtt, alog_col, shift, conv_w, conv_w, conv_w, conv_b, conv_b, conv_b,
      dsk_rep, norm_g)


def _dt_kernel(raw_ref, b_ref, o_ref):
    o_ref[...] = jax.nn.softplus(raw_ref[...] + b_ref[...]).T


def dt_transpose(raw, bias, tr=512):
    p = raw.shape[0]
    return pl.pallas_call(
        _dt_kernel,
        grid=(pl.cdiv(p, tr),),
        in_specs=[pl.BlockSpec((tr, LANES), lambda i: (i, 0)),
                  pl.BlockSpec((1, LANES), lambda i: (0, 0))],
        out_specs=pl.BlockSpec((LANES, tr), lambda i: (0, i)),
        out_shape=jax.ShapeDtypeStruct((LANES, p), F32),
        compiler_params=_params(("parallel",)),
        name="dt_transpose",
    )(raw, bias)


def _bias_selectors():
    h = jnp.arange(FOX_HEADS)
    r = jnp.arange(LANES)[None, :, None]
    c = jnp.arange(LANES)[None, None, :]
    hh = h[:, None, None]
    own = (r == hh + 32 * c) & (c < 3)
    ones_q = (r == 96) & (c >= 3) & (c < 6)
    sel_q = (own | ones_q).astype(BF16)
    own_k = (r == hh + 32 * (c - 3)) & (c >= 3) & (c < 6)
    ones_k = (r == 96) & (c < 3)
    sel_k = (own_k | ones_k).astype(BF16)
    return sel_q, sel_k


def kernel(x, meta, norm_g, fox_w_in, fox_b_f, fox_q_g, fox_k_g, fox_w_out, ssm_w_in, ssm_conv_w,
           ssm_conv_b, ssm_dt_bias, ssm_a_log, ssm_d, ssm_norm_g, ssm_w_out):
    p = SEQ + BLOCK
    prefix = jnp.concatenate([jnp.zeros((FRONT_PAD, D_MODEL), x.dtype), meta.astype(x.dtype)], axis=0)

    w_in = fox_w_in[0].astype(BF16)
    w_f = fox_w_in[0][:, 4 * FOX_WIDTH:]
    w_f = jnp.concatenate([w_f, w_f, w_f, jnp.zeros_like(w_f)], axis=1).astype(BF16)
    b_f = fox_b_f[0]
    b_f = jnp.concatenate([b_f, b_f, b_f, jnp.zeros_like(b_f)])[None, :]
    g0 = norm_g[0][None, :]
    hn, f_side = norm_side(prefix, g0, w_f, p, into=norm_side(x, g0, w_f, p), row0=SEQ)
    bound = (1.01 * FOX_HEAD_DIM ** 0.5) * jnp.max(jnp.abs(fox_q_g[0])) * jnp.max(jnp.abs(fox_k_g[0]))
    shift = bound * LOG2E
    cq, ck = fcum(f_side, b_f, jnp.full((1, LANES), shift, F32))
    sel_q, sel_k = _bias_selectors()
    tm, tn = 1376, 512
    q = mm_qk(hn, w_in, 0, fox_q_g[0][None, :], cq, sel_q, FOX_HEAD_DIM ** -0.5 * LOG2E, tm, tn)
    k = mm_qk(hn, w_in, FOX_WIDTH, fox_k_g[0][None, :], ck, sel_k, 1.0, tm, tn)
    v = mm_heads(hn, w_in, 2 * FOX_WIDTH, FOX_WIDTH, tm, tn, ones_pad=True)
    gate = mm_heads(hn, w_in, 3 * FOX_WIDTH, FOX_WIDTH, tm, tn)
    og = lax.cond(2.0 * shift <= SAFE_LOG2_RANGE,
                  functools.partial(fox_attention, seq=SEQ, bounded=True, bq=1024),
                  functools.partial(fox_attention, seq=SEQ, bounded=False, bq=512),
                  q, k, v, gate)
    w_out = fox_w_out[0].astype(BF16)
    h1 = mm_residual(og, w_out, x, SEQ, 1024, tn, FOX_WIDTH, out_rows=p)
    h1 = mm_residual(og, w_out, prefix, BLOCK, BLOCK, tn, FOX_WIDTH, out_rows=p, row0=SEQ,
                     into=h1, first_valid=FRONT_PAD)

    w_in = ssm_w_in[0]
    n_zx = 2 * SSM_INNER + 2 * SSM_GROUPS * SSM_STATE
    hn, dt_raw = norm_side(h1, norm_g[1][None, :], w_in[:, n_zx:].astype(BF16), p)
    dtt = dt_transpose(dt_raw, ssm_dt_bias[0][None, :])
    zx = mm_plain(hn, w_in.astype(BF16), n_zx, tm, tn)
    dsk_rep = jnp.repeat(ssm_d[0], SSM_HEAD_DIM)[None, :]
    y = ssd_scan(zx, dtt, ssm_a_log[0][:, None], ssm_conv_w[0], ssm_conv_b[0][None, :],
                 dsk_rep, ssm_norm_g[0][None, :])
    return mm_residual(y, ssm_w_out[0].astype(BF16), h1, SEQ, 1024, 512, 4096)
```

```python
import functools

import jax
import jax.numpy as jnp
from jax import lax
from jax.experimental import pallas as pl
from jax.experimental.pallas import tpu as pltpu

D_MODEL = 4096
SEQ = 16384
N_META = 16
BLOCK = 128
FRONT_PAD = BLOCK - N_META
EPS = 1e-6
NEG = -1e30

FOX_HEADS = 32
FOX_HEAD_DIM = 128
FOX_WIDTH = FOX_HEADS * FOX_HEAD_DIM

SSM_INNER = 8192
SSM_HEAD_DIM = 64
SSM_HEADS = SSM_INNER // SSM_HEAD_DIM
SSM_GROUPS = 8
SSM_HPG = SSM_HEADS // SSM_GROUPS
SSM_STATE = 128
CONV_WIDTH = 4
GROUP_WIDTH = SSM_HPG * SSM_HEAD_DIM

LOG2E = 1.4426950408889634
SAFE_LOG2_RANGE = 96.0

LANES = 128
VMEM_LIMIT = 56 * 1024 * 1024

TM_PROJ = (SEQ + BLOCK) // 12
TN_NORMED = 512
TN_PLAIN = 1024
ATTN_BQ, ATTN_BQ_ONLINE = 1024, 512
FOX_OUT_TILE = (2048, 256)
SSM_OUT_TILE = (1024, 256)

F32 = jnp.float32
BF16 = jnp.bfloat16


def _params(sem, vmem=VMEM_LIMIT):
    return pltpu.CompilerParams(dimension_semantics=sem, vmem_limit_bytes=vmem)


def _silu(x):
    h = 0.5 * x
    return h * (jnp.tanh(h) + 1.0)


def _split3(x):
    hi = x.astype(BF16)
    r1 = x - hi.astype(F32)
    mid = r1.astype(BF16)
    lo = (r1 - mid.astype(F32)).astype(BF16)
    return hi, mid, lo


def _norm_side_kernel(h_ref, g_ref, w_ref, *rest):
    hn_ref, side_ref = rest[-2:]
    x = h_ref[...]
    ms = jnp.mean(x * x, axis=-1, keepdims=True)
    hn = (x * lax.rsqrt(ms + EPS) * g_ref[...]).astype(BF16)
    hn_ref[...] = hn
    side_ref[...] = jnp.dot(hn, w_ref[...], preferred_element_type=F32)


def norm_side(h, g, w_side, rows, tr=1024, into=None, row0=0):
    n, d = h.shape[-2:]
    tr = min(tr, n)
    b0 = row0 // tr
    if h.ndim == 3:
        h_spec = pl.BlockSpec((None, tr, d), lambda i: (0, i, 0))
    else:
        h_spec = pl.BlockSpec((tr, d), lambda i: (i, 0))
    in_specs = [h_spec,
                pl.BlockSpec((1, d), lambda i: (0, 0)),
                pl.BlockSpec((d, LANES), lambda i: (0, 0))]
    args = [h, g, w_side]
    aliases = {}
    if into is not None:
        in_specs += [pl.BlockSpec(memory_space=pl.ANY)] * 2
        args += list(into)
        aliases = {3: 0, 4: 1}
    return pl.pallas_call(
        _norm_side_kernel,
        grid=(pl.cdiv(n, tr),),
        in_specs=in_specs,
        out_specs=[pl.BlockSpec((tr, d), lambda i: (b0 + i, 0)),
                   pl.BlockSpec((tr, LANES), lambda i: (b0 + i, 0))],
        out_shape=[jax.ShapeDtypeStruct((rows, d), BF16),
                   jax.ShapeDtypeStruct((rows, LANES), F32)],
        input_output_aliases=aliases,
        compiler_params=_params(("parallel",)),
        name="norm_side",
    )(*args)


def _fcum_kernel(side_ref, b_ref, shift_ref, tri_ref, cq_ref, ck_ref, carry_ref):
    i = pl.program_id(0)

    @pl.when(i == 0)
    def _():
        carry_ref[...] = jnp.zeros_like(carry_ref)

    row = lax.broadcasted_iota(jnp.int32, (BLOCK, LANES), 0)
    lane = lax.broadcasted_iota(jnp.int32, (BLOCK, LANES), 1)
    valid = jnp.logical_or(i > 0, row >= FRONT_PAD)
    lf = jax.nn.log_sigmoid(side_ref[...] + b_ref[...])
    lf = jnp.where(valid, lf, 0.0)
    hi, mid, lo = _split3(lf)
    tri = tri_ref[...]
    cum = (jnp.dot(tri, hi, preferred_element_type=F32)
           + jnp.dot(tri, mid, preferred_element_type=F32)
           + jnp.dot(tri, lo, preferred_element_type=F32)) + carry_ref[0:1, :]
    carry_ref[...] = jnp.broadcast_to(cum[BLOCK - 1:BLOCK, :], carry_ref.shape)

    def triple(v):
        vh, vm, vl = _split3(v)
        one = jnp.ones_like(vh)
        return jnp.where(lane < 32, vh, jnp.where(lane < 64, vm, jnp.where(lane < 96, vl, one)))

    cum2 = cum * LOG2E
    cq_ref[...] = triple(cum2 - shift_ref[...])
    ck_ref[...] = triple(jnp.where(valid, -cum2, NEG))


def fcum(side, b_rep, shift):
    p = side.shape[0]
    nblk = p // BLOCK
    tri = jnp.tril(jnp.ones((BLOCK, BLOCK), BF16))
    blk = lambda i: ((i + nblk - 1) % nblk, 0)
    return pl.pallas_call(
        _fcum_kernel,
        grid=(nblk,),
        in_specs=[pl.BlockSpec((BLOCK, LANES), blk),
                  pl.BlockSpec((1, LANES), lambda i: (0, 0)),
                  pl.BlockSpec((1, LANES), lambda i: (0, 0)),
                  pl.BlockSpec((BLOCK, BLOCK), lambda i: (0, 0))],
        out_specs=[pl.BlockSpec((BLOCK, LANES), blk),
                   pl.BlockSpec((BLOCK, LANES), blk)],
        out_shape=[jax.ShapeDtypeStruct((p, LANES), BF16),
                   jax.ShapeDtypeStruct((p, LANES), BF16)],
        scratch_shapes=[pltpu.VMEM((8, LANES), F32)],
        compiler_params=_params(("arbitrary",)),
        name="fcum",
    )(side, b_rep, shift, tri)


def _xw(x_ref, w_ref):
    return jnp.dot(x_ref[...], w_ref[...], preferred_element_type=F32)


def _mm_heads_kernel(x_ref, w_ref, o_ref):
    acc = _xw(x_ref, w_ref)
    for j in range(o_ref.shape[0]):
        o_ref[j] = acc[:, j * LANES:(j + 1) * LANES].astype(o_ref.dtype)


def mm_heads(x, w, col0, n, tm, tn):
    m, k = x.shape
    c0 = col0 // tn
    hb = tn // LANES
    return pl.pallas_call(
        _mm_heads_kernel,
        grid=(pl.cdiv(m, tm), n // tn),
        in_specs=[pl.BlockSpec((tm, k), lambda i, j: (i, 0)),
                  pl.BlockSpec((k, tn), lambda i, j: (0, c0 + j))],
        out_specs=pl.BlockSpec((hb, tm, LANES), lambda i, j: (j, i, 0)),
        out_shape=jax.ShapeDtypeStruct((n // LANES, m, LANES), BF16),
        compiler_params=_params(("parallel", "parallel")),
        name="mm_heads",
    )(x, w)


def _mm_qk_kernel(x_ref, w_ref, g_ref, c3_ref, sel_ref, o_ref, *, scale):
    acc = _xw(x_ref, w_ref)
    c3 = c3_ref[...]
    g = g_ref[...] * scale
    for j in range(o_ref.shape[0]):
        a = acc[:, j * LANES:(j + 1) * LANES]
        ms = jnp.mean(a * a, axis=-1, keepdims=True)
        o_ref[j, :, 0:LANES] = (a * lax.rsqrt(ms + EPS) * g).astype(BF16)
        o_ref[j, :, LANES:2 * LANES] = jnp.dot(
            c3, sel_ref[j], preferred_element_type=F32).astype(BF16)


def mm_qk(x, w, col0, g, c3, sel, scale, tm, tn):
    m, k = x.shape
    c0 = col0 // tn
    hb = tn // LANES
    nh = sel.shape[0]
    return pl.pallas_call(
        functools.partial(_mm_qk_kernel, scale=scale),
        grid=(pl.cdiv(m, tm), nh // hb),
        in_specs=[pl.BlockSpec((tm, k), lambda i, j: (i, 0)),
                  pl.BlockSpec((k, tn), lambda i, j: (0, c0 + j)),
                  pl.BlockSpec((1, LANES), lambda i, j: (0, 0)),
                  pl.BlockSpec((tm, LANES), lambda i, j: (i, 0)),
                  pl.BlockSpec((hb, LANES, LANES), lambda i, j: (j, 0, 0))],
        out_specs=pl.BlockSpec((hb, tm, 2 * LANES), lambda i, j: (j, i, 0)),
        out_shape=jax.ShapeDtypeStruct((nh, m, 2 * LANES), BF16),
        compiler_params=_params(("parallel", "parallel")),
        name="mm_qk",
    )(x, w, g, c3, sel)


def _mm_plain_kernel(x_ref, w_ref, o_ref):
    o_ref[...] = _xw(x_ref, w_ref).astype(o_ref.dtype)


def mm_plain(x, w, n, tm, tn, out_dtype=BF16):
    m, k = x.shape
    return pl.pallas_call(
        _mm_plain_kernel,
        grid=(pl.cdiv(m, tm), n // tn),
        in_specs=[pl.BlockSpec((tm, k), lambda i, j: (i, 0)),
                  pl.BlockSpec((k, tn), lambda i, j: (0, j))],
        out_specs=pl.BlockSpec((tm, tn), lambda i, j: (i, j)),
        out_shape=jax.ShapeDtypeStruct((m, n), out_dtype),
        compiler_params=_params(("parallel", "parallel")),
        name="mm_plain",
    )(x, w)


def _mm_res_kernel(x_ref, w_ref, r_ref, *rest, nk, first_valid):
    o_ref, acc_ref = rest[-2:]
    kk = pl.program_id(2)
    part = _xw(x_ref, w_ref)

    def finish(total):
        if first_valid:
            row = lax.broadcasted_iota(jnp.int32, total.shape, 0)
            total = jnp.where(row >= first_valid, total, 0.0)
        o_ref[...] = r_ref[...] + total

    if nk == 1:
        finish(part)
    else:
        @pl.when(kk == 0)
        def _():
            acc_ref[...] = part

        @pl.when(jnp.logical_and(kk > 0, kk < nk - 1))
        def _():
            acc_ref[...] += part

        @pl.when(kk == nk - 1)
        def _():
            finish(acc_ref[...] + part)


def mm_residual(x, w, res, m, tm, tn, tk, *, out_rows=None, row0=0, into=None, first_valid=0):
    k = x.shape[1]
    n = w.shape[1]
    nk = k // tk
    b0 = row0 // tm
    if res.ndim == 3:
        res_spec = pl.BlockSpec((None, tm, tn), lambda i, j, kk: (0, i, j))
    else:
        res_spec = pl.BlockSpec((tm, tn), lambda i, j, kk: (i, j))
    in_specs = [pl.BlockSpec((tm, tk), lambda i, j, kk: (b0 + i, kk)),
                pl.BlockSpec((tk, tn), lambda i, j, kk: (kk, j)),
                res_spec]
    args = [x, w, res]
    aliases = {}
    if into is not None:
        in_specs.append(pl.BlockSpec(memory_space=pl.ANY))
        args.append(into)
        aliases = {3: 0}
    if out_rows is None:
        out_spec = pl.BlockSpec((None, tm, tn), lambda i, j, kk: (0, i, j))
        out_shape = jax.ShapeDtypeStruct((1, m, n), F32)
    else:
        out_spec = pl.BlockSpec((tm, tn), lambda i, j, kk: (b0 + i, j))
        out_shape = jax.ShapeDtypeStruct((out_rows, n), F32)
    return pl.pallas_call(
        functools.partial(_mm_res_kernel, nk=nk, first_valid=first_valid),
        grid=(m // tm, n // tn, nk),
        in_specs=in_specs,
        out_specs=out_spec,
        out_shape=out_shape,
        input_output_aliases=aliases,
        scratch_shapes=[pltpu.VMEM((tm, tn), F32)],
        compiler_params=_params(("parallel", "parallel", "arbitrary")),
        name="mm_residual",
    )(*args)


_NT = (((1,), (1,)), ((), ()))


def _gated_store(o_ref, gate_ref, num, den):
    den = jnp.where(den > 0.0, den, 1.0)
    o_ref[...] = (num / den * _silu(gate_ref[0].astype(F32))).astype(o_ref.dtype)


def _attn_bounded_kernel(q_ref, qn_ref, k_ref, v_ref, gate_ref, o_ref, acc_ref, p_ref, *,
                         seq, bq, bk, nq):
    i = pl.program_id(1)
    sub = bq // bk

    def probs(kc, mask):
        t = lax.dot_general(q_ref[0], kc, _NT, preferred_element_type=F32)
        if mask is not None:
            t = jnp.where(mask, t, NEG)
        return jnp.exp2(t).astype(BF16)

    def chunk_probs(n, mask):
        return probs(k_ref[0, pl.ds(pl.multiple_of(n * bk, bk), bk), :], mask)

    def with_ones(vc):
        return jnp.concatenate([vc, jnp.ones_like(vc)], axis=1)

    def chunk_pv(p, n):
        return jnp.dot(p, with_ones(v_ref[0, pl.ds(pl.multiple_of(n * bk, bk), bk), :]),
                       preferred_element_type=F32)

    def causal(n):
        col_minus_row = (lax.broadcasted_iota(jnp.int32, (bq, bk), 1)
                         - lax.broadcasted_iota(jnp.int32, (bq, bk), 0))
        return col_minus_row <= i * bq - n * bk

    def span(first, count):
        p = p_ref[...]
        tot = None
        for u in range(count):
            nxt = first + u + 1
            p_next = chunk_probs(nxt, causal(nxt) if u == count - 1 else None)
            r = chunk_pv(p, first + u)
            tot = r if tot is None else tot + r
            p = p_next
        p_ref[...] = p
        acc_ref[...] += tot

    kp = k_ref[0, seq:seq + BLOCK, :]
    vp = with_ones(v_ref[0, seq:seq + BLOCK, :])

    def finish():
        acc = acc_ref[...]
        _gated_store(o_ref, gate_ref, acc[:, 0:LANES], acc[:, LANES:LANES + 1])

    @pl.when(i == 0)
    def _():
        p_ref[...] = chunk_probs(0, causal(0))

    @pl.when(i < nq)
    def _():
        acc_ref[...] = jnp.zeros_like(acc_ref)

        def body(j, carry):
            span(j * (4 * sub), 4 * sub)
            return carry

        lax.fori_loop(0, i // 4, body, 0)

        @pl.when((i // 2) % 2 == 1)
        def _():
            span((i // 4) * (4 * sub), 2 * sub)

        @pl.when(i % 2 == 1)
        def _():
            span((i - 1) * sub, sub)

        tot = jnp.dot(probs(kp, None), vp, preferred_element_type=F32)
        tot = tot + chunk_pv(p_ref[...], i * sub)
        tri = (lax.broadcasted_iota(jnp.int32, (bk, bk), 1)
               <= lax.broadcasted_iota(jnp.int32, (bk, bk), 0))
        parts = []
        for jj in range(1, sub):
            n = i * sub + jj
            kc = k_ref[0, pl.ds(pl.multiple_of(n * bk, bk), bk), :]
            t = lax.dot_general(q_ref[0, jj * bk:bq, :], kc, _NT, preferred_element_type=F32)
            top = jnp.where(tri, t[0:bk], NEG)
            t = top if jj == sub - 1 else jnp.concatenate([top, t[bk:]], axis=0)
            parts.append(chunk_pv(jnp.exp2(t).astype(BF16), n))
        t = lax.dot_general(qn_ref[0], k_ref[0, 0:bk, :], _NT, preferred_element_type=F32)
        p_ref[...] = jnp.exp2(t).astype(BF16)
        acc_ref[...] += tot
        for jj in range(1, sub):
            acc_ref[jj * bk:bq, :] += parts[jj - 1]
        finish()

    @pl.when(i == nq)
    def _():
        tri = (lax.broadcasted_iota(jnp.int32, (bq, BLOCK), 1)
               <= lax.broadcasted_iota(jnp.int32, (bq, BLOCK), 0))
        acc_ref[...] = jnp.dot(probs(kp, tri), vp, preferred_element_type=F32)
        finish()


def _attn_online_kernel(q_ref, k_ref, v_ref, gate_ref, o_ref, *, seq, bq, nq):
    i = pl.program_id(1)
    q = q_ref[0]

    def update(carry, kc, vc, mask):
        m, l, acc = carry
        s = lax.dot_general(q, kc, _NT, preferred_element_type=F32)
        if mask is not None:
            s = jnp.where(mask, s, NEG)
        m_new = jnp.maximum(m, jnp.max(s, axis=-1, keepdims=True))
        alpha = jnp.exp2(m - m_new)
        p = jnp.exp2(s - m_new)
        l = alpha * l + jnp.sum(p, axis=-1, keepdims=True)
        acc = alpha * acc + jnp.dot(p.astype(BF16), vc, preferred_element_type=F32)
        return m_new, l, acc

    def finish(carry):
        _, l, acc = carry
        _gated_store(o_ref, gate_ref, acc, l)

    init = (jnp.full((bq, 1), NEG, F32), jnp.zeros((bq, 1), F32),
            jnp.zeros((bq, FOX_HEAD_DIM), F32))
    kp = k_ref[0, seq:seq + BLOCK, :]
    vp = v_ref[0, seq:seq + BLOCK, :]

    @pl.when(i < nq)
    def _():
        carry = update(init, kp, vp, None)

        def body(j, c):
            off = pl.multiple_of(j * bq, bq)
            return update(c, k_ref[0, pl.ds(off, bq), :], v_ref[0, pl.ds(off, bq), :], None)

        carry = lax.fori_loop(0, i, body, carry)
        off = pl.multiple_of(i * bq, bq)
        tri = (lax.broadcasted_iota(jnp.int32, (bq, bq), 1)
               <= lax.broadcasted_iota(jnp.int32, (bq, bq), 0))
        carry = update(carry, k_ref[0, pl.ds(off, bq), :], v_ref[0, pl.ds(off, bq), :], tri)
        finish(carry)

    @pl.when(i == nq)
    def _():
        tri = (lax.broadcasted_iota(jnp.int32, (bq, BLOCK), 1)
               <= lax.broadcasted_iota(jnp.int32, (bq, BLOCK), 0))
        finish(update(init, kp, vp, tri))


def fox_attention(q, k, v, gate, seq, bounded, bq, bk=256, gate_head0=0):
    nh, p, _ = q.shape
    nq = seq // bq
    q_spec = pl.BlockSpec((1, bq, 2 * LANES), lambda h, i: (h, i, 0))
    if bounded:
        body = functools.partial(_attn_bounded_kernel, seq=seq, bq=bq, bk=bk, nq=nq)
        scratch = [pltpu.VMEM((bq, 2 * LANES), F32), pltpu.VMEM((bq, bk), BF16)]
        q_specs = [q_spec, pl.BlockSpec((1, bq, 2 * LANES),
                                        lambda h, i: (h, jnp.minimum(i + 1, nq - 1), 0))]
        q_args = (q, q)
    else:
        body = functools.partial(_attn_online_kernel, seq=seq, bq=bq, nq=nq)
        scratch = []
        q_specs = [q_spec]
        q_args = (q,)
    return pl.pallas_call(
        body,
        grid=(nh, nq + 1),
        in_specs=q_specs + [pl.BlockSpec((1, p, 2 * LANES), lambda h, i: (h, 0, 0)),
                            pl.BlockSpec((1, p, LANES), lambda h, i: (h, 0, 0)),
                            pl.BlockSpec((1, bq, LANES), lambda h, i: (gate_head0 + h, i, 0))],
        out_specs=pl.BlockSpec((bq, LANES), lambda h, i: (i, h)),
        out_shape=jax.ShapeDtypeStruct((p, nh * LANES), BF16),
        scratch_shapes=scratch,
        compiler_params=_params(("parallel", "arbitrary")),
        name="fox_attention_bounded" if bounded else "fox_attention_online",
    )(*q_args, k, v, gate)


def _conv_silu(raw_ref, prev_ref, slot, shift_ref, w_ref, b_ref, valid):
    raw = raw_ref[...]
    u = jnp.concatenate([prev_ref[1 - slot], raw], axis=0)
    sh = jnp.dot(shift_ref[...], u, preferred_element_type=F32)
    prev_ref[slot] = raw
    acc = b_ref[...] + w_ref[CONV_WIDTH - 1:CONV_WIDTH, :] * raw.astype(F32)
    for s in range(1, CONV_WIDTH):
        acc = acc + w_ref[CONV_WIDTH - 1 - s:CONV_WIDTH - s, :] * sh[(s - 1) * BLOCK:s * BLOCK, :]
    return jnp.where(valid, _silu(acc), 0.0)


def _ssd_kernel(z_ref, x_ref, b_ref, c_ref, dtt_ref, alog_ref, shift_ref, wx_ref, wb_ref, wc_ref,
                bx_ref, bb_ref, bc_ref, dsk_ref, ng_ref, y_ref,
                st_ref, y_scr, px_ref, pb_ref, pc_ref, *, gps):
    c = pl.program_id(1)

    @pl.when(c == 0)
    def _():
        st_ref[...] = jnp.zeros_like(st_ref)
        px_ref[...] = jnp.zeros_like(px_ref)
        pb_ref[...] = jnp.zeros_like(pb_ref)
        pc_ref[...] = jnp.zeros_like(pc_ref)

    row1 = lax.broadcasted_iota(jnp.int32, (BLOCK, 1), 0)
    valid = jnp.logical_or(c > 0, row1 >= FRONT_PAD)
    slot = c % 2
    xs_all = _conv_silu(x_ref, px_ref, slot, shift_ref, wx_ref, bx_ref, valid)
    bm_all = _conv_silu(b_ref, pb_ref, slot, shift_ref, wb_ref, bb_ref, valid)
    cm_all = _conv_silu(c_ref, pc_ref, slot, shift_ref, wc_ref, bc_ref, valid)

    dtt_all = dtt_ref[...]
    dat = dtt_all * (-jnp.exp(alog_ref[...]) * LOG2E)
    upper = (lax.broadcasted_iota(jnp.int32, (BLOCK, BLOCK), 0)
             <= lax.broadcasted_iota(jnp.int32, (BLOCK, BLOCK), 1)).astype(BF16)
    hi, mid, lo = _split3(dat)
    cumt_all = (jnp.dot(hi, upper, preferred_element_type=F32)
                + jnp.dot(mid, upper, preferred_element_type=F32)
                + jnp.dot(lo, upper, preferred_element_type=F32))
    last = cumt_all[:, BLOCK - 1:BLOCK]
    wdt_all = jnp.exp2(last - cumt_all) * dtt_all

    causal = (lax.broadcasted_iota(jnp.int32, (BLOCK, BLOCK), 1)
              <= lax.broadcasted_iota(jnp.int32, (BLOCK, BLOCK), 0))
    lane = lax.broadcasted_iota(jnp.int32, (BLOCK, LANES), 1)
    left = lane < SSM_HEAD_DIM

    for gi in range(gps):
        gcols = slice(gi * GROUP_WIDTH, (gi + 1) * GROUP_WIDTH)
        ncols = slice(gi * SSM_STATE, (gi + 1) * SSM_STATE)
        hrows = slice(gi * SSM_HPG, (gi + 1) * SSM_HPG)
        xs = xs_all[:, gcols]
        bm = bm_all[:, ncols]
        cm = cm_all[:, ncols]
        cumt, wdt = cumt_all[hrows], wdt_all[hrows]
        l2dt = jnp.log2(dtt_all[hrows])
        cumt_dt = cumt - l2dt
        xs_b = xs.astype(BF16)
        cb = lax.dot_general(cm.astype(BF16), bm.astype(BF16), _NT, preferred_element_type=F32)
        cb = jnp.where(causal, cb, 0.0)
        bmt_f = bm.T

        for pr in range(SSM_HPG // 2):
            lhs_y = []
            lhs_s = []
            elast = []
            for hh in (2 * pr, 2 * pr + 1):
                rh = jnp.broadcast_to(cumt[hh:hh + 1, :], (BLOCK, BLOCK))
                ch = rh.T
                decay_dt = jnp.exp2(jnp.minimum(ch - cumt_dt[hh:hh + 1, :], l2dt[hh:hh + 1, :]))
                ech = jnp.exp2(ch)
                lhs_y.append(jnp.concatenate([(cb * decay_dt).astype(BF16),
                                              (cm * ech).astype(BF16)], axis=1))
                lhs_s.append((bmt_f * wdt[hh:hh + 1, :]).astype(BF16))
                elast.append(ech[BLOCK - 1:BLOCK, :])
            cols = slice(gi * GROUP_WIDTH + pr * LANES, gi * GROUP_WIDTH + (pr + 1) * LANES)
            st_old = st_ref[:, cols]
            xs_pair = xs_b[:, pr * LANES:(pr + 1) * LANES]
            ry = jnp.dot(jnp.concatenate(lhs_y, axis=0),
                         jnp.concatenate([xs_pair, st_old.astype(BF16)], axis=0),
                         preferred_element_type=F32)
            rs = jnp.dot(jnp.concatenate(lhs_s, axis=0), xs_pair, preferred_element_type=F32)
            y_scr[:, cols] = jnp.where(left, ry[0:BLOCK], ry[BLOCK:2 * BLOCK])
            ds = jnp.where(left, rs[0:BLOCK], rs[BLOCK:2 * BLOCK])
            el = jnp.where(left[0:1, :], elast[0], elast[1])
            st_ref[:, cols] = st_old * el + ds

        z = z_ref[:, gcols].astype(F32)
        y = y_scr[:, gcols] + dsk_ref[:, gcols] * xs
        y = y * _silu(z)
        y = y * lax.rsqrt(jnp.mean(y * y, axis=-1, keepdims=True) + EPS)
        y_ref[:, gcols] = (y * ng_ref[:, gcols]).astype(y_ref.dtype)


def ssd_scan(zx, dtt, alog_col, conv_w, conv_b, dsk_rep, norm_g, gps=4):
    p = zx.shape[0]
    nch = p // BLOCK
    gw = gps * GROUP_WIDTH
    nw = gps * SSM_STATE
    hw = gps * SSM_HPG
    zb = SSM_INNER // gw
    bb = 2 * SSM_INNER // nw
    cb = bb + SSM_GROUPS // gps
    cwx = 0
    cwb = SSM_INNER // nw
    cwc = cwb + SSM_GROUPS // gps
    rows = lambda c: (c + nch - 1) % nch
    t = jnp.arange(BLOCK)[None, :, None]
    s = jnp.arange(1, CONV_WIDTH)[:, None, None]
    shift = (jnp.arange(2 * BLOCK)[None, None, :] == BLOCK + t - s).astype(BF16)
    shift = shift.reshape((CONV_WIDTH - 1) * BLOCK, 2 * BLOCK)
    return pl.pallas_call(
        functools.partial(_ssd_kernel, gps=gps),
        grid=(SSM_GROUPS // gps, nch),
        in_specs=[
            pl.BlockSpec((BLOCK, gw), lambda g, c: (rows(c), g)),
            pl.BlockSpec((BLOCK, gw), lambda g, c: (rows(c), zb + g)),
            pl.BlockSpec((BLOCK, nw), lambda g, c: (rows(c), bb + g)),
            pl.BlockSpec((BLOCK, nw), lambda g, c: (rows(c), cb + g)),
            pl.BlockSpec((hw, BLOCK), lambda g, c: (g, rows(c))),
            pl.BlockSpec((hw, 1), lambda g, c: (g, 0)),
            pl.BlockSpec(((CONV_WIDTH - 1) * BLOCK, 2 * BLOCK), lambda g, c: (0, 0)),
            pl.BlockSpec((CONV_WIDTH, gw), lambda g, c: (0, cwx + g)),
            pl.BlockSpec((CONV_WIDTH, nw), lambda g, c: (0, cwb + g)),
            pl.BlockSpec((CONV_WIDTH, nw), lambda g, c: (0, cwc + g)),
            pl.BlockSpec((1, gw), lambda g, c: (0, cwx + g)),
            pl.BlockSpec((1, nw), lambda g, c: (0, cwb + g)),
            pl.BlockSpec((1, nw), lambda g, c: (0, cwc + g)),
            pl.BlockSpec((1, gw), lambda g, c: (0, g)),
            pl.BlockSpec((1, gw), lambda g, c: (0, g)),
        ],
        out_specs=pl.BlockSpec((BLOCK, gw), lambda g, c: (rows(c), g)),
        out_shape=jax.ShapeDtypeStruct((p, SSM_INNER), BF16),
        scratch_shapes=[pltpu.VMEM((SSM_STATE, gw), F32),
                        pltpu.VMEM((BLOCK, gw), F32),
                        pltpu.VMEM((2, BLOCK, gw), BF16),
                        pltpu.VMEM((2, BLOCK, nw), BF16),
                        pltpu.VMEM((2, BLOCK, nw), BF16)],
        compiler_params=_params(("parallel", "arbitrary")),
        name="ssd_scan",
    )(zx, zx, zx, zx, dtt, alog_col, shift, conv_w, conv_w, conv_w, conv_b, conv_b, conv_b,
      dsk_rep, norm_g)


def _dt_kernel(raw_ref, b_ref, o_ref):
    o_ref[...] = jax.nn.softplus(raw_ref[...] + b_ref[...]).T


def dt_transpose(raw, bias, tr=512):
    p = raw.shape[0]
    return pl.pallas_call(
        _dt_kernel,
        grid=(pl.cdiv(p, tr),),
        in_specs=[pl.BlockSpec((tr, LANES), lambda i: (i, 0)),
                  pl.BlockSpec((1, LANES), lambda i: (0, 0))],
        out_specs=pl.BlockSpec((LANES, tr), lambda i: (0, i)),
        out_shape=jax.ShapeDtypeStruct((LANES, p), F32),
        compiler_params=_params(("parallel",)),
        name="dt_transpose",
    )(raw, bias)


def _bias_selectors():
    h = jnp.arange(FOX_HEADS)
    r = jnp.arange(LANES)[None, :, None]
    c = jnp.arange(LANES)[None, None, :]
    hh = h[:, None, None]
    own = (r == hh + 32 * c) & (c < 3)
    ones_q = (r == 96) & (c >= 3) & (c < 6)
    sel_q = (own | ones_q).astype(BF16)
    own_k = (r == hh + 32 * (c - 3)) & (c >= 3) & (c < 6)
    ones_k = (r == 96) & (c < 3)
    sel_k = (own_k | ones_k).astype(BF16)
    return sel_q, sel_k


def kernel(x, meta, norm_g, fox_w_in, fox_b_f, fox_q_g, fox_k_g, fox_w_out, ssm_w_in, ssm_conv_w,
           ssm_conv_b, ssm_dt_bias, ssm_a_log, ssm_d, ssm_norm_g, ssm_w_out):
    p = SEQ + BLOCK
    prefix = jnp.concatenate([jnp.zeros((FRONT_PAD, D_MODEL), x.dtype), meta.astype(x.dtype)], axis=0)

    w_in = fox_w_in[0].astype(BF16)
    w_f = fox_w_in[0][:, 4 * FOX_WIDTH:]
    w_f = jnp.concatenate([w_f, w_f, w_f, jnp.zeros_like(w_f)], axis=1).astype(BF16)
    b_f = fox_b_f[0]
    b_f = jnp.concatenate([b_f, b_f, b_f, jnp.zeros_like(b_f)])[None, :]
    g0 = norm_g[0][None, :]
    hn, f_side = norm_side(prefix, g0, w_f, p, into=norm_side(x, g0, w_f, p), row0=SEQ)
    bound = (1.01 * FOX_HEAD_DIM ** 0.5) * jnp.max(jnp.abs(fox_q_g[0])) * jnp.max(jnp.abs(fox_k_g[0]))
    shift = bound * LOG2E
    cq, ck = fcum(f_side, b_f, jnp.full((1, LANES), shift, F32))
    sel_q, sel_k = _bias_selectors()
    q = mm_qk(hn, w_in, 0, fox_q_g[0][None, :], cq, sel_q, FOX_HEAD_DIM ** -0.5 * LOG2E,
              TM_PROJ, TN_NORMED)
    k = mm_qk(hn, w_in, FOX_WIDTH, fox_k_g[0][None, :], ck, sel_k, 1.0, TM_PROJ, TN_NORMED)
    vg = mm_heads(hn, w_in, 2 * FOX_WIDTH, 2 * FOX_WIDTH, TM_PROJ, TN_PLAIN)
    og = lax.cond(2.0 * shift <= SAFE_LOG2_RANGE,
                  functools.partial(fox_attention, seq=SEQ, bounded=True, bq=ATTN_BQ,
                                    gate_head0=FOX_HEADS),
                  functools.partial(fox_attention, seq=SEQ, bounded=False, bq=ATTN_BQ_ONLINE,
                                    gate_head0=FOX_HEADS),
                  q, k, vg, vg)
    w_out = fox_w_out[0].astype(BF16)
    h1 = mm_residual(og, w_out, x, SEQ, *FOX_OUT_TILE, FOX_WIDTH, out_rows=p)
    h1 = mm_residual(og, w_out, prefix, BLOCK, BLOCK, FOX_OUT_TILE[1], FOX_WIDTH, out_rows=p,
                     row0=SEQ, into=h1, first_valid=FRONT_PAD)

    w_in = ssm_w_in[0]
    n_zx = 2 * SSM_INNER + 2 * SSM_GROUPS * SSM_STATE
    hn, dt_raw = norm_side(h1, norm_g[1][None, :], w_in[:, n_zx:].astype(BF16), p)
    dtt = dt_transpose(dt_raw, ssm_dt_bias[0][None, :])
    zx = mm_plain(hn, w_in.astype(BF16), n_zx, TM_PROJ, TN_PLAIN)
    dsk_rep = jnp.repeat(ssm_d[0], SSM_HEAD_DIM)[None, :]
    y = ssd_scan(zx, dtt, ssm_a_log[0][:, None], ssm_conv_w[0], ssm_conv_b[0][None, :],
                 dsk_rep, ssm_norm_g[0][None, :])
    return mm_residual(y, ssm_w_out[0].astype(BF16), h1, SEQ, *SSM_OUT_TILE, SSM_INNER)
```
